```python
import math
import jax, jax.numpy as jnp
from jax import lax
import numpy as np

D_MODEL = 2048
BATCH = 4
SEQ = 4096
DEPTH = 2

N_MIXERS = 2
N_MLSTM = (DEPTH + 1) // 2
N_GMLP = DEPTH // 2
PROJ_FACTOR = 2
MLSTM_INNER = PROJ_FACTOR * D_MODEL
MLSTM_HEADS = 8
MLSTM_VD = MLSTM_INNER // MLSTM_HEADS
MLSTM_QKD = MLSTM_VD // 2
MLSTM_QK = MLSTM_HEADS * MLSTM_QKD
MLSTM_CHUNK = 64
CONV_K = 4
GMLP_INNER = PROJ_FACTOR * D_MODEL
GMLP_GROUPS = 8
GMLP_GD = GMLP_INNER // GMLP_GROUPS
GMLP_CHUNK = 128
EPS = 1e-6
MLSTM_IN_COLS = 2 * MLSTM_QK + 3 * MLSTM_INNER + 2 * MLSTM_HEADS
GMLP_IN_COLS = 3 * GMLP_INNER

kernel_name = "hybrid_mlstm_gmlp_trunk"


def rmsnorm(x, g):
    xf = x.astype(jnp.float32)
    y = xf * lax.rsqrt(jnp.mean(xf * xf, axis=-1, keepdims=True) + EPS)
    return (y * g.astype(jnp.float32)).astype(x.dtype)


def causal_depthwise_conv(x, w, b):
    k = w.shape[0]
    s = x.shape[1]
    xp = jnp.pad(x, ((0, 0), (k - 1, 0), (0, 0)))
    out = b
    for j in range(k):
        out = out + w[j] * xp[:, j:j + s]
    return out


def _mlstm_chunk(carry, inp):
    c_st, n_st, m_st = carry
    q, k, v, ig, lf = inp
    L = q.shape[2]
    b = jnp.cumsum(lf, axis=-1)
    g = b[..., -1]
    causal = jnp.tril(jnp.ones((L, L), dtype=bool))
    dmat = jnp.where(causal, b[..., :, None] - b[..., None, :] + ig[..., None, :], -jnp.inf)
    inter = b + m_st[..., None]
    m_t = jnp.maximum(inter, jnp.max(dmat, axis=-1))
    s = jnp.einsum('bhtd,bhsd->bhts', q, k) * jnp.exp(dmat - m_t[..., None])
    a = jnp.exp(inter - m_t)
    num = a[..., None] * jnp.einsum('bhtd,bhde->bhte', q, c_st) + jnp.einsum('bhts,bhse->bhte', s, v)
    den = a * jnp.einsum('bhtd,bhd->bht', q, n_st) + jnp.sum(s, axis=-1)
    h = num / jnp.maximum(jnp.abs(den), jnp.exp(-m_t))[..., None]
    w = g[..., None] - b + ig
    m_new = jnp.maximum(g + m_st, jnp.max(w, axis=-1))
    wexp = jnp.exp(w - m_new[..., None])
    decay = jnp.exp(g + m_st - m_new)
    c_new = decay[..., None, None] * c_st + jnp.einsum('bhsd,bhse->bhde', k * wexp[..., None], v)
    n_new = decay[..., None] * n_st + jnp.einsum('bhs,bhsd->bhd', wexp, k)
    return (c_new, n_new, m_new), h


def mlstm_cell(q, k, v, ig, lf):
    bsz, s, h, dk = q.shape
    dv = v.shape[-1]
    nc = s // MLSTM_CHUNK

    def to_chunks(t):
        t = t.reshape((bsz, nc, MLSTM_CHUNK, h) + t.shape[3:])
        return jnp.moveaxis(jnp.moveaxis(t, 1, 0), 3, 2)

    init = (jnp.zeros((bsz, h, dk, dv), jnp.float32),
            jnp.zeros((bsz, h, dk), jnp.float32),
            jnp.zeros((bsz, h), jnp.float32))
    _, hs = lax.scan(_mlstm_chunk, init,
                     (to_chunks(q), to_chunks(k), to_chunks(v), to_chunks(ig), to_chunks(lf)))
    return jnp.transpose(hs, (1, 0, 3, 2, 4)).reshape(bsz, s, h, dv)


def mlstm_layer(xn, w_in, conv_w, conv_b, gate_b, head_g, w_out):
    bsz, s, _ = xn.shape
    p = xn @ w_in
    o1 = 2 * MLSTM_QK
    o2 = o1 + MLSTM_INNER
    o3 = o2 + MLSTM_INNER
    o4 = o3 + MLSTM_INNER
    qk = jax.nn.silu(causal_depthwise_conv(p[..., :o1], conv_w, conv_b))
    q = qk[..., :MLSTM_QK].reshape(bsz, s, MLSTM_HEADS, MLSTM_QKD)
    k = qk[..., MLSTM_QK:].reshape(bsz, s, MLSTM_HEADS, MLSTM_QKD)
    v = p[..., o1:o2].reshape(bsz, s, MLSTM_HEADS, MLSTM_VD)
    o = p[..., o2:o3].reshape(bsz, s, MLSTM_HEADS, MLSTM_VD)
    z = p[..., o3:o4]
    gates = (p[..., o4:] + gate_b).astype(jnp.float32)
    ig = gates[..., :MLSTM_HEADS]
    lf = jax.nn.log_sigmoid(gates[..., MLSTM_HEADS:])
    f32 = jnp.float32
    hc = mlstm_cell(q.astype(f32) * (MLSTM_QKD ** -0.5), k.astype(f32), v.astype(f32), ig, lf)
    hc = hc * jax.nn.sigmoid(o.astype(f32))
    hc = hc * lax.rsqrt(jnp.mean(hc * hc, axis=-1, keepdims=True) + EPS)
    hc = hc.reshape(bsz, s, MLSTM_INNER) * head_g.astype(f32)
    y = hc.astype(xn.dtype) * jax.nn.silu(z)
    return y @ w_out


def gmlp_layer(xn, w_in, ln_g, ln_b, w_s, b_s, w_out):
    bsz, s, _ = xn.shape
    p = xn @ w_in
    u = jax.nn.gelu(p[..., :GMLP_INNER], approximate=False)
    v = jax.nn.gelu(p[..., GMLP_INNER:2 * GMLP_INNER], approximate=False)
    z = p[..., 2 * GMLP_INNER:]
    vf = v.astype(jnp.float32)
    mu = jnp.mean(vf, axis=-1, keepdims=True)
    var = jnp.mean(jnp.square(vf - mu), axis=-1, keepdims=True)
    vf = (vf - mu) * lax.rsqrt(var + EPS) * ln_g.astype(jnp.float32) + ln_b.astype(jnp.float32)
    v = vf.astype(xn.dtype)
    nch = s // GMLP_CHUNK
    v = v.reshape(bsz, nch, GMLP_CHUNK, GMLP_GROUPS, GMLP_GD)
    mask = jnp.tril(jnp.ones((GMLP_CHUNK, GMLP_CHUNK), dtype=w_s.dtype))
    ws = w_s * mask
    sv = jnp.einsum('gts,bcsgd->bctgd', ws, v) + jnp.transpose(b_s)[None, None, :, :, None]
    sv = sv.reshape(bsz, s, GMLP_INNER)
    y = u * sv * jax.nn.silu(z)
    return y @ w_out


def setup_inputs(seed: int = 0) -> dict:
    key = jax.random.key(seed)
    ks = jax.random.split(key, 20)
    f32 = jnp.float32
    nrm = lambda k, shp, sc: jax.random.normal(k, shp, f32) * sc
    x = jax.random.normal(ks[0], (BATCH, SEQ, D_MODEL), f32)
    mlstm_norm_g = 1.0 + nrm(ks[1], (N_MLSTM, D_MODEL), 0.02)
    mlstm_w_in = nrm(ks[2], (N_MLSTM, D_MODEL, MLSTM_IN_COLS), D_MODEL ** -0.5)
    mlstm_conv_w = nrm(ks[3], (N_MLSTM, CONV_K, 2 * MLSTM_QK), CONV_K ** -0.5)
    mlstm_conv_b = nrm(ks[4], (N_MLSTM, 2 * MLSTM_QK), 0.01)
    ig_b = nrm(ks[5], (N_MLSTM, MLSTM_HEADS), 0.1)
    fg_b = 3.0 + nrm(ks[6], (N_MLSTM, MLSTM_HEADS), 0.5)
    mlstm_gate_b = jnp.concatenate([ig_b, fg_b], axis=-1)
    mlstm_head_g = 1.0 + nrm(ks[7], (N_MLSTM, MLSTM_INNER), 0.02)
    mlstm_w_out = nrm(ks[8], (N_MLSTM, MLSTM_INNER, D_MODEL), MLSTM_INNER ** -0.5)
    gmlp_norm_g = 1.0 + nrm(ks[9], (N_GMLP, D_MODEL), 0.02)
    gmlp_w_in = nrm(ks[10], (N_GMLP, D_MODEL, GMLP_IN_COLS), D_MODEL ** -0.5)
    gmlp_ln_g = 1.0 + nrm(ks[11], (N_GMLP, GMLP_INNER), 0.02)
    gmlp_ln_b = nrm(ks[12], (N_GMLP, GMLP_INNER), 0.01)
    gmlp_w_s = nrm(ks[13], (N_GMLP, GMLP_GROUPS, GMLP_CHUNK, GMLP_CHUNK), GMLP_CHUNK ** -0.5)
    gmlp_b_s = 1.0 + nrm(ks[14], (N_GMLP, GMLP_GROUPS, GMLP_CHUNK), 0.01)
    gmlp_w_out = nrm(ks[15], (N_GMLP, GMLP_INNER, D_MODEL), GMLP_INNER ** -0.5)
    final_norm_g = 1.0 + nrm(ks[16], (D_MODEL,), 0.02)
    return {"x": x,
            "mlstm_norm_g": mlstm_norm_g, "mlstm_w_in": mlstm_w_in,
            "mlstm_conv_w": mlstm_conv_w, "mlstm_conv_b": mlstm_conv_b,
            "mlstm_gate_b": mlstm_gate_b, "mlstm_head_g": mlstm_head_g,
            "mlstm_w_out": mlstm_w_out,
            "gmlp_norm_g": gmlp_norm_g, "gmlp_w_in": gmlp_w_in,
            "gmlp_ln_g": gmlp_ln_g, "gmlp_ln_b": gmlp_ln_b,
            "gmlp_w_s": gmlp_w_s, "gmlp_b_s": gmlp_b_s, "gmlp_w_out": gmlp_w_out,
            "final_norm_g": final_norm_g}


def reference(x, mlstm_norm_g, mlstm_w_in, mlstm_conv_w, mlstm_conv_b, mlstm_gate_b,
              mlstm_head_g, mlstm_w_out, gmlp_norm_g, gmlp_w_in, gmlp_ln_g, gmlp_ln_b,
              gmlp_w_s, gmlp_b_s, gmlp_w_out, final_norm_g):
    h = x
    for i in range(DEPTH):
        j = i // N_MIXERS
        if i % N_MIXERS == 0:
            xn = rmsnorm(h, mlstm_norm_g[j])
            h = h + mlstm_layer(xn, mlstm_w_in[j], mlstm_conv_w[j], mlstm_conv_b[j],
                                mlstm_gate_b[j], mlstm_head_g[j], mlstm_w_out[j])
        else:
            xn = rmsnorm(h, gmlp_norm_g[j])
            h = h + gmlp_layer(xn, gmlp_w_in[j], gmlp_ln_g[j], gmlp_ln_b[j],
                               gmlp_w_s[j], gmlp_b_s[j], gmlp_w_out[j])
    return rmsnorm(h, final_norm_g)
```

```python
import functools
import math

import jax
import jax.numpy as jnp
from jax import lax
from jax.experimental import pallas as pl
from jax.experimental.pallas import tpu as pltpu

EPS = 1e-6
F32 = jnp.float32
BF16 = jnp.bfloat16
LANES = 128
SUBLANES = 8
VMEM_LIMIT_BYTES = 56 * 1024 * 1024
MLSTM_CELL_CHUNK = 256


def _params(*sem):
    return pltpu.CompilerParams(dimension_semantics=sem, vmem_limit_bytes=VMEM_LIMIT_BYTES)


def _sigmoid(x):
    return 1.0 / (1.0 + jnp.exp(-x))


def _silu(x):
    return x * _sigmoid(x)


def _gelu_exact(x):
    return 0.5 * x * (1.0 + lax.erf(x * math.sqrt(0.5)))


def _log_sigmoid(x):
    return jnp.minimum(x, 0.0) - jnp.log1p(jnp.exp(-jnp.abs(x)))


def _rmsnorm_kernel(x_ref, g_ref, o_ref):
    x = x_ref[...]
    y = x * lax.rsqrt(jnp.mean(x * x, axis=-1, keepdims=True) + EPS)
    o_ref[...] = (y * g_ref[...]).astype(o_ref.dtype)


def _rmsnorm(x, g, *, tm, out_dtype):
    t, d = x.shape
    return pl.pallas_call(
        _rmsnorm_kernel,
        grid=(t // tm,),
        in_specs=[pl.BlockSpec((tm, d), lambda i: (i, 0)),
                  pl.BlockSpec((1, d), lambda i: (0, 0))],
        out_specs=pl.BlockSpec((tm, d), lambda i: (i, 0)),
        out_shape=jax.ShapeDtypeStruct((t, d), out_dtype),
        compiler_params=_params("arbitrary"),
        name="rmsnorm",
    )(x, g.reshape(1, d))


def _matmul_kernel(x_ref, w_ref, o_ref, *, n_gelu_tiles):
    acc = jnp.dot(x_ref[...], w_ref[...], preferred_element_type=F32)
    if n_gelu_tiles == 0:
        o_ref[...] = acc.astype(o_ref.dtype)
        return
    j = pl.program_id(1)

    @pl.when(j < n_gelu_tiles)
    def _():
        o_ref[...] = _gelu_exact(acc).astype(o_ref.dtype)

    @pl.when(j >= n_gelu_tiles)
    def _():
        o_ref[...] = acc.astype(o_ref.dtype)


def _matmul(x, w, n_out, *, tm, tn, n_gelu_cols=0, out_dtype=BF16, name):
    t, k = x.shape
    kern = functools.partial(_matmul_kernel, n_gelu_tiles=n_gelu_cols // tn)
    return pl.pallas_call(
        kern,
        grid=(t // tm, n_out // tn),
        in_specs=[pl.BlockSpec((tm, k), lambda i, j: (i, 0)),
                  pl.BlockSpec((k, tn), lambda i, j: (0, j))],
        out_specs=pl.BlockSpec((tm, tn), lambda i, j: (i, j)),
        out_shape=jax.ShapeDtypeStruct((t, n_out), out_dtype),
        compiler_params=_params("arbitrary", "arbitrary"),
        name=name,
    )(x, w)


def _gate_kernel(xn_ref, wg_ref, gb_ref, bc_ref, cc_ref, cr_ref):
    gates = jnp.dot(xn_ref[...], wg_ref[...], preferred_element_type=F32) + gb_ref[...]
    lf = _log_sigmoid(gates[:, :LANES])
    ig = gates[:, LANES:]
    rows = lf.shape[0]
    r = lax.broadcasted_iota(jnp.int32, (rows, rows), 0)
    c = lax.broadcasted_iota(jnp.int32, (rows, rows), 1)
    tri = (r >= c).astype(F32)
    b = jnp.dot(tri, lf, preferred_element_type=F32, precision=lax.Precision.HIGHEST)
    cval = b - ig
    bc_ref[...] = b
    cc_ref[...] = cval
    cr_ref[...] = cval.T


def _mlstm_gates(xn, wg, gb, *, chunk):
    t, d = xn.shape
    col = pl.BlockSpec((chunk, LANES), lambda i: (i, 0))
    return pl.pallas_call(
        _gate_kernel,
        grid=(t // chunk,),
        in_specs=[pl.BlockSpec((chunk, d), lambda i: (i, 0)),
                  pl.BlockSpec((d, 2 * LANES), lambda i: (0, 0)),
                  pl.BlockSpec((1, 2 * LANES), lambda i: (0, 0))],
        out_specs=[col, col, pl.BlockSpec((LANES, chunk), lambda i: (0, i))],
        out_shape=[jax.ShapeDtypeStruct((t, LANES), F32),
                   jax.ShapeDtypeStruct((t, LANES), F32),
                   jax.ShapeDtypeStruct((LANES, t), F32)],
        compiler_params=_params("arbitrary"),
        name="mlstm_gates",
    )(xn, wg, gb)


def _mlstm_cell_kernel(qk_ref, v_ref, o_ref, z_ref, bc_ref, cc_ref, cr_ref,
                       cw_ref, cb_ref, hg_ref, y_ref,
                       xp_ref, c_ref, n_ref, m_ref, *, heads, dk, dv, conv_k):
    L = qk_ref.shape[0]
    qk_w = heads * dk
    halo = SUBLANES

    @pl.when(pl.program_id(1) == 0)
    def _():
        xp_ref[0:halo, :] = jnp.zeros((halo, xp_ref.shape[1]), F32)
        c_ref[...] = jnp.zeros(c_ref.shape, F32)
        n_ref[...] = jnp.zeros(n_ref.shape, F32)
        m_ref[...] = jnp.zeros(m_ref.shape, F32)

    xp_ref[halo:halo + L, :] = qk_ref[...].astype(F32)

    r = lax.broadcasted_iota(jnp.int32, (L, L), 0)
    c = lax.broadcasted_iota(jnp.int32, (L, L), 1)
    causal = r >= c
    scale = dk ** -0.5

    def conv(cols):
        acc = cb_ref[:, cols]
        for j in range(conv_k):
            start = halo - (conv_k - 1) + j
            acc = acc + cw_ref[j:j + 1, cols] * xp_ref[start:start + L, cols]
        return acc

    for h in range(heads):
        qcols = slice(h * dk, (h + 1) * dk)
        kcols = slice(qk_w + h * dk, qk_w + (h + 1) * dk)
        vcols = slice(h * dv, (h + 1) * dv)
        q = (_silu(conv(qcols)) * scale).astype(BF16)
        kf = _silu(conv(kcols))
        v = v_ref[:, vcols]

        bcol = bc_ref[:, h:h + 1]
        ccol = cc_ref[:, h:h + 1]
        crow = cr_ref[h:h + 1, :]
        m = m_ref[h:h + 1, 0:1]
        g = bcol[L - 1:L, :]

        dmat = jnp.where(causal, bcol - crow, -jnp.inf)
        inter = bcol + m
        m_t = jnp.maximum(inter, jnp.max(dmat, axis=-1, keepdims=True))
        s = lax.dot_general(q, kf.astype(BF16), (((1,), (1,)), ((), ())),
                            preferred_element_type=F32) * jnp.exp(dmat - m_t)
        a = jnp.exp(inter - m_t)
        c_st = c_ref[h]
        num = (a * jnp.dot(q, c_st.astype(BF16), preferred_element_type=F32)
               + jnp.dot(s.astype(BF16), v, preferred_element_type=F32))
        qn = jnp.sum(q.astype(F32) * n_ref[h:h + 1, :], axis=-1, keepdims=True)
        den = a * qn + jnp.sum(s, axis=-1, keepdims=True)
        hc = num * (1.0 / jnp.maximum(jnp.abs(den), jnp.exp(-m_t)))

        m_new = jnp.maximum(g + m, jnp.max(g - crow, axis=-1, keepdims=True))
        kw = kf * jnp.exp((g - ccol) - m_new)
        decay = jnp.exp(g + m - m_new)
        c_ref[h] = decay * c_st + lax.dot_general(
            kw.astype(BF16), v, (((0,), (0,)), ((), ())), preferred_element_type=F32)
        n_ref[h:h + 1, :] = decay * n_ref[h:h + 1, :] + jnp.sum(kw, axis=0, keepdims=True)
        m_ref[h:h + 1, :] = jnp.broadcast_to(m_new, (1, m_ref.shape[1]))

        hc = hc * _sigmoid(o_ref[:, vcols].astype(F32))
        hc = hc * lax.rsqrt(jnp.mean(hc * hc, axis=-1, keepdims=True) + EPS)
        hc = hc * hg_ref[:, vcols]
        y_ref[:, vcols] = (hc * _silu(z_ref[:, vcols].astype(F32))).astype(y_ref.dtype)

    xp_ref[0:halo, :] = xp_ref[L:L + halo, :]


def _mlstm_cell(p, bc, cc, cr, conv_w, conv_b, head_g, *, batch, seq, heads, dk, dv, chunk):
    t = p.shape[0]
    qk_w = heads * dk
    inner = heads * dv
    assert 2 * qk_w == inner, "q|k block must be as wide as the v block"
    nc = seq // chunk
    conv_k = conv_w.shape[0]
    row = lambda b, c: b * nc + c
    wide = lambda j: pl.BlockSpec((chunk, inner), lambda b, c: (row(b, c), j))
    col = pl.BlockSpec((chunk, LANES), lambda b, c: (row(b, c), 0))
    full = lambda shape: pl.BlockSpec(shape, lambda b, c: (0,) * len(shape))
    kern = functools.partial(_mlstm_cell_kernel, heads=heads, dk=dk, dv=dv, conv_k=conv_k)
    return pl.pallas_call(
        kern,
        grid=(batch, nc),
        in_specs=[wide(0), wide(1), wide(2), wide(3), col, col,
                  pl.BlockSpec((LANES, chunk), lambda b, c: (0, row(b, c))),
                  full((conv_k, inner)), full((1, inner)), full((1, inner))],
        out_specs=pl.BlockSpec((chunk, inner), lambda b, c: (row(b, c), 0)),
        out_shape=jax.ShapeDtypeStruct((t, inner), BF16),
        scratch_shapes=[pltpu.VMEM((chunk + SUBLANES, inner), F32),
                        pltpu.VMEM((heads, dk, dv), F32),
                        pltpu.VMEM((heads, dk), F32),
                        pltpu.VMEM((heads, LANES), F32)],
        compiler_params=_params("arbitrary", "arbitrary"),
        name="mlstm_cell",
    )(p, p, p, p, bc, cc, cr, conv_w, conv_b.reshape(1, inner), head_g.reshape(1, inner))


def _outproj_kernel(y_ref, w_ref, r_ref, g_ref, *out_refs, emit_residual):
    h = r_ref[...] + jnp.dot(y_ref[...], w_ref[...], preferred_element_type=F32)
    xn = h * lax.rsqrt(jnp.mean(h * h, axis=-1, keepdims=True) + EPS) * g_ref[...]
    if emit_residual:
        h_ref, xn_ref = out_refs
        h_ref[...] = h
    else:
        (xn_ref,) = out_refs
    xn_ref[...] = xn.astype(xn_ref.dtype)


def _outproj(y, w, resid, g, *, tm, emit_residual, norm_dtype, name):
    t, k = y.shape
    d = w.shape[1]
    row = lambda width: pl.BlockSpec((tm, width), lambda i: (i, 0))
    out_specs = [row(d)]
    out_shape = [jax.ShapeDtypeStruct((t, d), norm_dtype)]
    if emit_residual:
        out_specs = [row(d)] + out_specs
        out_shape = [jax.ShapeDtypeStruct((t, d), F32)] + out_shape
    return pl.pallas_call(
        functools.partial(_outproj_kernel, emit_residual=emit_residual),
        grid=(t // tm,),
        in_specs=[row(k), pl.BlockSpec((k, d), lambda i: (0, 0)), row(d),
                  pl.BlockSpec((1, d), lambda i: (0, 0))],
        out_specs=out_specs,
        out_shape=out_shape,
        compiler_params=_params("arbitrary"),
        name=name,
    )(y, w, resid, g.reshape(1, d))


def _gmlp_mix_kernel(u_ref, v_ref, z_ref, lg_ref, lb_ref, ws_ref, bs_ref, y_ref, vn_ref,
                     *, groups):
    tm, inner = v_ref.shape
    ch = ws_ref.shape[1]
    gd = inner // groups
    v = v_ref[...].astype(F32)
    mu = jnp.mean(v, axis=-1, keepdims=True)
    d = v - mu
    var = jnp.mean(d * d, axis=-1, keepdims=True)
    vn_ref[...] = (d * lax.rsqrt(var + EPS) * lg_ref[...] + lb_ref[...]).astype(vn_ref.dtype)

    r = lax.broadcasted_iota(jnp.int32, (ch, ch), 0)
    c = lax.broadcasted_iota(jnp.int32, (ch, ch), 1)
    mask = (r >= c).astype(F32)
    for g in range(groups):
        ws = (ws_ref[g] * mask).astype(BF16)
        cols = slice(g * gd, (g + 1) * gd)
        for k in range(tm // ch):
            rows = slice(k * ch, (k + 1) * ch)
            sv = jnp.dot(ws, vn_ref[rows, cols], preferred_element_type=F32) + bs_ref[:, g:g + 1]
            u = u_ref[rows, cols].astype(F32)
            y_ref[rows, cols] = (u * sv * _silu(z_ref[rows, cols].astype(F32))).astype(y_ref.dtype)


def _gmlp_mix(p, ln_g, ln_b, w_s, b_s_t, *, tm):
    t = p.shape[0]
    groups, ch, _ = w_s.shape
    inner = p.shape[1] // 3
    wide = lambda j: pl.BlockSpec((tm, inner), lambda i: (i, j))
    full = lambda shape: pl.BlockSpec(shape, lambda i: (0,) * len(shape))
    return pl.pallas_call(
        functools.partial(_gmlp_mix_kernel, groups=groups),
        grid=(t // tm,),
        in_specs=[wide(0), wide(1), wide(2), full((1, inner)), full((1, inner)),
                  full((groups, ch, ch)), full((ch, groups))],
        out_specs=pl.BlockSpec((tm, inner), lambda i: (i, 0)),
        out_shape=jax.ShapeDtypeStruct((t, inner), BF16),
        scratch_shapes=[pltpu.VMEM((tm, inner), BF16)],
        compiler_params=_params("arbitrary"),
        name="gmlp_mix",
    )(p, p, p, ln_g.reshape(1, inner), ln_b.reshape(1, inner), w_s, b_s_t)


def _pick(n, pref):
    tile = min(n, pref)
    while n % tile:
        tile //= 2
    return tile


def _mlstm_layer(h2d, norm_g, w_in, conv_w, conv_b, gate_b, head_g, w_out, next_g,
                 *, batch, seq):
    t, d = h2d.shape
    heads = gate_b.shape[0] // 2
    qk_w = conv_w.shape[1] // 2
    inner = head_g.shape[0]
    dk, dv = qk_w // heads, inner // heads
    n_main = 2 * qk_w + 3 * inner
    chunk = _pick(seq, MLSTM_CELL_CHUNK)

    xn = _rmsnorm(h2d, norm_g, tm=_pick(t, 512), out_dtype=BF16)
    w_bf = w_in.astype(BF16)
    p = _matmul(xn, w_bf, n_main, tm=_pick(t, 1024), tn=_pick(n_main, 1024), name="mlstm_inproj")

    wg = jnp.zeros((d, 2 * LANES), BF16)
    wg = wg.at[:, 0:heads].set(w_bf[:, n_main + heads:n_main + 2 * heads])
    wg = wg.at[:, LANES:LANES + heads].set(w_bf[:, n_main:n_main + heads])
    gb = jnp.zeros((1, 2 * LANES), F32)
    gb = gb.at[0, 0:heads].set(gate_b[heads:]).at[0, LANES:LANES + heads].set(gate_b[:heads])
    bc, cc, cr = _mlstm_gates(xn, wg, gb, chunk=chunk)

    y = _mlstm_cell(p, bc, cc, cr, conv_w, conv_b, head_g, batch=batch, seq=seq,
                    heads=heads, dk=dk, dv=dv, chunk=chunk)
    return _outproj(y, w_out.astype(BF16), h2d, next_g, tm=_pick(t, 256),
                    emit_residual=True, norm_dtype=BF16, name="mlstm_outproj")


def _gmlp_layer(h2d, xn, w_in, ln_g, ln_b, w_s, b_s, w_out, final_g):
    t, d = h2d.shape
    inner = ln_g.shape[0]
    ch = w_s.shape[1]
    p = _matmul(xn, w_in.astype(BF16), 3 * inner, tm=_pick(t, 1024), tn=_pick(inner, 1024),
                n_gelu_cols=2 * inner, name="gmlp_inproj")
    y = _gmlp_mix(p, ln_g, ln_b, w_s, jnp.transpose(b_s), tm=max(ch, _pick(t, 256)))
    (out,) = _outproj(y, w_out.astype(BF16), h2d, final_g, tm=_pick(t, 256),
                      emit_residual=False, norm_dtype=F32, name="gmlp_outproj")
    return out


def kernel(x, mlstm_norm_g, mlstm_w_in, mlstm_conv_w, mlstm_conv_b, mlstm_gate_b, mlstm_head_g,
           mlstm_w_out, gmlp_norm_g, gmlp_w_in, gmlp_ln_g, gmlp_ln_b, gmlp_w_s, gmlp_b_s,
           gmlp_w_out, final_norm_g):
    batch, seq, d = x.shape
    assert mlstm_w_in.shape[0] == 1 and gmlp_w_in.shape[0] == 1, "one mLSTM and one gMLP layer"
    h0 = x.reshape(batch * seq, d)
    h1, xn1 = _mlstm_layer(h0, mlstm_norm_g[0], mlstm_w_in[0], mlstm_conv_w[0], mlstm_conv_b[0],
                           mlstm_gate_b[0], mlstm_head_g[0], mlstm_w_out[0], gmlp_norm_g[0],
                           batch=batch, seq=seq)
    out = _gmlp_layer(h1, xn1, gmlp_w_in[0], gmlp_ln_g[0], gmlp_ln_b[0], gmlp_w_s[0],
                      gmlp_b_s[0], gmlp_w_out[0], final_norm_g)
    return out.reshape(batch, seq, d)
```

```python
import functools
import math

import jax
import jax.numpy as jnp
from jax import lax
from jax.experimental import pallas as pl
from jax.experimental.pallas import tpu as pltpu

EPS = 1e-6
F32 = jnp.float32
BF16 = jnp.bfloat16
LANES = 128
SUBLANES = 8
BF16_ROWS = 16
VMEM_LIMIT_BYTES = 56 * 1024 * 1024
MLSTM_CELL_CHUNK = 256
PROJ_TILE = 1024


def _params(*sem):
    return pltpu.CompilerParams(dimension_semantics=sem, vmem_limit_bytes=VMEM_LIMIT_BYTES)


def _sigmoid(x):
    return 1.0 / (1.0 + jnp.exp(-x))


def _silu(x):
    return x * _sigmoid(x)


def _gelu_exact(x):
    return 0.5 * x * (1.0 + lax.erf(x * math.sqrt(0.5)))


def _log_sigmoid(x):
    return jnp.minimum(x, 0.0) - jnp.log1p(jnp.exp(-jnp.abs(x)))


_ACTIVATIONS = {"none": lambda x: x, "sigmoid": _sigmoid, "silu": _silu, "gelu": _gelu_exact}


def _pick(n, pref):
    tile = min(n, pref)
    while n % tile:
        tile //= 2
    return tile


def _norm_gate_kernel(x_ref, g_ref, wg_ref, gb_ref, xn_ref, bc_ref, cc_ref, cr_ref):
    x = x_ref[...]
    xn = (x * lax.rsqrt(jnp.mean(x * x, axis=-1, keepdims=True) + EPS) * g_ref[...]).astype(BF16)
    xn_ref[...] = xn
    gates = jnp.dot(xn, wg_ref[...], preferred_element_type=F32) + gb_ref[...]
    lf = _log_sigmoid(gates[:, :LANES])
    ig = gates[:, LANES:]
    rows = lf.shape[0]
    r = lax.broadcasted_iota(jnp.int32, (rows, rows), 0)
    c = lax.broadcasted_iota(jnp.int32, (rows, rows), 1)
    tri = (r >= c).astype(F32)
    b = jnp.dot(tri, lf, preferred_element_type=F32, precision=lax.Precision.HIGHEST)
    cval = b - ig
    bc_ref[...] = b
    cc_ref[...] = cval
    cr_ref[...] = cval.T


def _norm_gates(x, g, wg, gb, *, chunk):
    t, d = x.shape
    col = pl.BlockSpec((chunk, LANES), lambda i: (i, 0))
    row = pl.BlockSpec((chunk, d), lambda i: (i, 0))
    return pl.pallas_call(
        _norm_gate_kernel,
        grid=(t // chunk,),
        in_specs=[row, pl.BlockSpec((1, d), lambda i: (0, 0)),
                  pl.BlockSpec((d, 2 * LANES), lambda i: (0, 0)),
                  pl.BlockSpec((1, 2 * LANES), lambda i: (0, 0))],
        out_specs=[row, col, col, pl.BlockSpec((LANES, chunk), lambda i: (0, i))],
        out_shape=[jax.ShapeDtypeStruct((t, d), BF16),
                   jax.ShapeDtypeStruct((t, LANES), F32),
                   jax.ShapeDtypeStruct((t, LANES), F32),
                   jax.ShapeDtypeStruct((LANES, t), F32)],
        compiler_params=_params("arbitrary"),
        name="norm_gates",
    )(x, g.reshape(1, d), wg, gb)


def _proj_kernel(x_ref, w_ref, o_ref, wbf_ref, *, act):
    @pl.when(pl.program_id(1) == 0)
    def _():
        wbf_ref[...] = w_ref[...].astype(BF16)

    acc = jnp.dot(x_ref[...], wbf_ref[...], preferred_element_type=F32)
    o_ref[...] = _ACTIVATIONS[act](acc).astype(o_ref.dtype)


def _proj(x, w, col0, n_out, *, act, name):
    t, k = x.shape
    tm, tn = _pick(t, PROJ_TILE), _pick(n_out, PROJ_TILE)
    assert col0 % tn == 0
    jb = col0 // tn
    return pl.pallas_call(
        functools.partial(_proj_kernel, act=act),
        grid=(n_out // tn, t // tm),
        in_specs=[pl.BlockSpec((tm, k), lambda j, i: (i, 0)),
                  pl.BlockSpec((k, tn), lambda j, i: (0, jb + j))],
        out_specs=pl.BlockSpec((tm, tn), lambda j, i: (i, j)),
        out_shape=jax.ShapeDtypeStruct((t, n_out), BF16),
        scratch_shapes=[pltpu.VMEM((k, tn), BF16)],
        compiler_params=_params("arbitrary", "arbitrary"),
        name=name,
    )(x, w)


def _proj_conv_kernel(x_ref, xh_ref, w_ref, cw_ref, cb_ref, o_ref, wbf_ref, cs_ref,
                      *, seq, q_tiles, q_scale):
    j, i = pl.program_id(0), pl.program_id(1)
    tm = x_ref.shape[0]
    conv_k = cw_ref.shape[0]
    halo = SUBLANES

    @pl.when(i == 0)
    def _():
        wbf_ref[...] = w_ref[...].astype(BF16)

    w = wbf_ref[...]
    cs_ref[halo:halo + tm, :] = jnp.dot(x_ref[...], w, preferred_element_type=F32)
    prev = jnp.dot(xh_ref[...], w, preferred_element_type=F32)[-halo:, :]
    cs_ref[0:halo, :] = jnp.where((i * tm) % seq == 0, 0.0, prev)
    acc = cb_ref[...] + cw_ref[conv_k - 1:conv_k, :] * cs_ref[halo:halo + tm, :]
    for tap in range(conv_k - 1):
        start = halo - (conv_k - 1) + tap
        acc = acc + cw_ref[tap:tap + 1, :] * cs_ref[start:start + tm, :]
    scale = jnp.where(j < q_tiles, q_scale, 1.0).astype(F32)
    o_ref[...] = (_silu(acc) * scale).astype(o_ref.dtype)


def _proj_conv(x, w, conv_w, conv_b, *, seq, q_scale, name):
    t, k = x.shape
    conv_k, n_out = conv_w.shape
    tm, tn = _pick(seq, PROJ_TILE), _pick(n_out // 2, PROJ_TILE)
    hb = tm // BF16_ROWS
    kern = functools.partial(_proj_conv_kernel, seq=seq, q_tiles=n_out // 2 // tn, q_scale=q_scale)
    return pl.pallas_call(
        kern,
        grid=(n_out // tn, t // tm),
        in_specs=[pl.BlockSpec((tm, k), lambda j, i: (i, 0)),
                  pl.BlockSpec((BF16_ROWS, k), lambda j, i: (jnp.maximum(i * hb - 1, 0), 0)),
                  pl.BlockSpec((k, tn), lambda j, i: (0, j)),
                  pl.BlockSpec((conv_k, tn), lambda j, i: (0, j)),
                  pl.BlockSpec((1, tn), lambda j, i: (0, j))],
        out_specs=pl.BlockSpec((tm, tn), lambda j, i: (i, j)),
        out_shape=jax.ShapeDtypeStruct((t, n_out), BF16),
        scratch_shapes=[pltpu.VMEM((k, tn), BF16),
                        pltpu.VMEM((tm + SUBLANES, tn), F32)],
        compiler_params=_params("arbitrary", "arbitrary"),
        name=name,
    )(x, x, w, conv_w, conv_b.reshape(1, n_out))


def _mlstm_cell_kernel(qk_ref, v_ref, so_ref, sz_ref, bc_ref, cc_ref, cr_ref, hg_ref, y_ref,
                       c_ref, n_ref, m_ref, *, heads, dk, dv):
    L = qk_ref.shape[0]
    qk_w = heads * dk

    @pl.when(pl.program_id(1) == 0)
    def _():
        c_ref[...] = jnp.zeros(c_ref.shape, F32)
        n_ref[...] = jnp.zeros(n_ref.shape, F32)
        m_ref[...] = jnp.zeros(m_ref.shape, F32)

    r = lax.broadcasted_iota(jnp.int32, (L, L), 0)
    c = lax.broadcasted_iota(jnp.int32, (L, L), 1)
    causal = r >= c

    for h in range(heads):
        vcols = slice(h * dv, (h + 1) * dv)
        q = qk_ref[:, h * dk:(h + 1) * dk]
        k = qk_ref[:, qk_w + h * dk:qk_w + (h + 1) * dk]
        v = v_ref[:, vcols]

        bcol = bc_ref[:, h:h + 1]
        ccol = cc_ref[:, h:h + 1]
        crow = cr_ref[h:h + 1, :]
        m = m_ref[h:h + 1, 0:1]
        g = bcol[L - 1:L, :]

        dmat = jnp.where(causal, bcol - crow, -jnp.inf)
        inter = bcol + m
        m_t = jnp.maximum(inter, jnp.max(dmat, axis=-1, keepdims=True))
        s = lax.dot_general(q, k, (((1,), (1,)), ((), ())),
                            preferred_element_type=F32) * jnp.exp(dmat - m_t)
        a = jnp.exp(inter - m_t)
        c_st = c_ref[h]
        num = (a * jnp.dot(q, c_st.astype(BF16), preferred_element_type=F32)
               + jnp.dot(s.astype(BF16), v, preferred_element_type=F32))
        qn = jnp.sum(q.astype(F32) * n_ref[h:h + 1, :], axis=-1, keepdims=True)
        den = a * qn + jnp.sum(s, axis=-1, keepdims=True)
        hc = num * (1.0 / jnp.maximum(jnp.abs(den), jnp.exp(-m_t)))

        m_new = jnp.maximum(g + m, jnp.max(g - crow, axis=-1, keepdims=True))
        kw = k.astype(F32) * jnp.exp((g - ccol) - m_new)
        decay = jnp.exp(g + m - m_new)
        c_ref[h] = decay * c_st + lax.dot_general(
            kw.astype(BF16), v, (((0,), (0,)), ((), ())), preferred_element_type=F32)
        n_ref[h:h + 1, :] = decay * n_ref[h:h + 1, :] + jnp.sum(kw, axis=0, keepdims=True)
        m_ref[h:h + 1, :] = jnp.broadcast_to(m_new, (1, m_ref.shape[1]))

        hc = hc * so_ref[:, vcols].astype(F32)
        hc = hc * lax.rsqrt(jnp.mean(hc * hc, axis=-1, keepdims=True) + EPS)
        hc = hc * hg_ref[:, vcols]
        y_ref[:, vcols] = (hc * sz_ref[:, vcols].astype(F32)).astype(y_ref.dtype)


def _mlstm_cell(qk, v, so, sz, bc, cc, cr, head_g, *, batch, seq, heads, chunk):
    t, inner = v.shape
    qk_w = qk.shape[1] // 2
    dk, dv = qk_w // heads, inner // heads
    nc = seq // chunk
    row = lambda b, c: b * nc + c
    wide = lambda width: pl.BlockSpec((chunk, width), lambda b, c: (row(b, c), 0))
    col = pl.BlockSpec((chunk, LANES), lambda b, c: (row(b, c), 0))
    kern = functools.partial(_mlstm_cell_kernel, heads=heads, dk=dk, dv=dv)
    return pl.pallas_call(
        kern,
        grid=(batch, nc),
        in_specs=[wide(2 * qk_w), wide(inner), wide(inner), wide(inner), col, col,
                  pl.BlockSpec((LANES, chunk), lambda b, c: (0, row(b, c))),
                  pl.BlockSpec((1, inner), lambda b, c: (0, 0))],
        out_specs=wide(inner),
        out_shape=jax.ShapeDtypeStruct((t, inner), BF16),
        scratch_shapes=[pltpu.VMEM((heads, dk, dv), F32),
                        pltpu.VMEM((heads, dk), F32),
                        pltpu.VMEM((heads, LANES), F32)],
        compiler_params=_params("arbitrary", "arbitrary"),
        name="mlstm_cell",
    )(qk, v, so, sz, bc, cc, cr, head_g.reshape(1, inner))


def _outproj_kernel(y_ref, w_ref, r_ref, g_ref, *out_refs, emit_residual):
    h = r_ref[...] + jnp.dot(y_ref[...], w_ref[...], preferred_element_type=F32)
    xn = h * lax.rsqrt(jnp.mean(h * h, axis=-1, keepdims=True) + EPS) * g_ref[...]
    if emit_residual:
        h_ref, xn_ref = out_refs
        h_ref[...] = h
    else:
        (xn_ref,) = out_refs
    xn_ref[...] = xn.astype(xn_ref.dtype)


def _outproj(y, w, resid, g, *, tm, emit_residual, norm_dtype, name):
    t, k = y.shape
    d = w.shape[1]
    row = lambda width: pl.BlockSpec((tm, width), lambda i: (i, 0))
    out_specs = [row(d)]
    out_shape = [jax.ShapeDtypeStruct((t, d), norm_dtype)]
    if emit_residual:
        out_specs = [row(d)] + out_specs
        out_shape = [jax.ShapeDtypeStruct((t, d), F32)] + out_shape
    return pl.pallas_call(
        functools.partial(_outproj_kernel, emit_residual=emit_residual),
        grid=(t // tm,),
        in_specs=[row(k), pl.BlockSpec((k, d), lambda i: (0, 0)), row(d),
                  pl.BlockSpec((1, d), lambda i: (0, 0))],
        out_specs=out_specs,
        out_shape=out_shape,
        compiler_params=_params("arbitrary"),
        name=name,
    )(y, w, resid, g.reshape(1, d))


def _gmlp_mix_kernel(u_ref, v_ref, sz_ref, lg_ref, lb_ref, ws_ref, bs_ref, y_ref, vn_ref,
                     *, groups):
    tm, inner = v_ref.shape
    ch = ws_ref.shape[1]
    gd = inner // groups
    v = v_ref[...].astype(F32)
    mu = jnp.mean(v, axis=-1, keepdims=True)
    d = v - mu
    var = jnp.mean(d * d, axis=-1, keepdims=True)
    vn_ref[...] = (d * lax.rsqrt(var + EPS) * lg_ref[...] + lb_ref[...]).astype(vn_ref.dtype)

    r = lax.broadcasted_iota(jnp.int32, (ch, ch), 0)
    c = lax.broadcasted_iota(jnp.int32, (ch, ch), 1)
    mask = (r >= c).astype(F32)
    for g in range(groups):
        ws = (ws_ref[g] * mask).astype(BF16)
        cols = slice(g * gd, (g + 1) * gd)
        for k in range(tm // ch):
            rows = slice(k * ch, (k + 1) * ch)
            sv = jnp.dot(ws, vn_ref[rows, cols], preferred_element_type=F32) + bs_ref[:, g:g + 1]
            u = u_ref[rows, cols].astype(F32)
            y_ref[rows, cols] = (u * sv * sz_ref[rows, cols].astype(F32)).astype(y_ref.dtype)


def _gmlp_mix(uv, sz, ln_g, ln_b, w_s, b_s_t, *, tm):
    t, inner = sz.shape
    groups, ch, _ = w_s.shape
    wide = lambda j: pl.BlockSpec((tm, inner), lambda i: (i, j))
    full = lambda shape: pl.BlockSpec(shape, lambda i: (0,) * len(shape))
    return pl.pallas_call(
        functools.partial(_gmlp_mix_kernel, groups=groups),
        grid=(t // tm,),
        in_specs=[wide(0), wide(1), wide(0), full((1, inner)), full((1, inner)),
                  full((groups, ch, ch)), full((ch, groups))],
        out_specs=pl.BlockSpec((tm, inner), lambda i: (i, 0)),
        out_shape=jax.ShapeDtypeStruct((t, inner), BF16),
        scratch_shapes=[pltpu.VMEM((tm, inner), BF16)],
        compiler_params=_params("arbitrary"),
        name="gmlp_mix",
    )(uv, uv, sz, ln_g.reshape(1, inner), ln_b.reshape(1, inner), w_s, b_s_t)


def _mlstm_layer(h2d, norm_g, w_in, conv_w, conv_b, gate_b, head_g, w_out, next_g,
                 *, batch, seq):
    t, d = h2d.shape
    heads = gate_b.shape[0] // 2
    qk_w = conv_w.shape[1] // 2
    inner = head_g.shape[0]
    dk = qk_w // heads
    n_main = 2 * qk_w + 3 * inner
    chunk = _pick(seq, MLSTM_CELL_CHUNK)

    w_gate = w_in[:, n_main:].astype(BF16)
    wg = jnp.zeros((d, 2 * LANES), BF16)
    wg = wg.at[:, 0:heads].set(w_gate[:, heads:]).at[:, LANES:LANES + heads].set(w_gate[:, :heads])
    gb = jnp.zeros((1, 2 * LANES), F32)
    gb = gb.at[0, 0:heads].set(gate_b[heads:]).at[0, LANES:LANES + heads].set(gate_b[:heads])
    xn, bc, cc, cr = _norm_gates(h2d, norm_g, wg, gb, chunk=chunk)

    qk = _proj_conv(xn, w_in, conv_w, conv_b, seq=seq, q_scale=dk ** -0.5, name="mlstm_proj_qk")
    v = _proj(xn, w_in, 2 * qk_w, inner, act="none", name="mlstm_proj_v")
    so = _proj(xn, w_in, 2 * qk_w + inner, inner, act="sigmoid", name="mlstm_proj_o")
    sz = _proj(xn, w_in, 2 * qk_w + 2 * inner, inner, act="silu", name="mlstm_proj_z")

    y = _mlstm_cell(qk, v, so, sz, bc, cc, cr, head_g, batch=batch, seq=seq, heads=heads,
                    chunk=chunk)
    return _outproj(y, w_out.astype(BF16), h2d, next_g, tm=_pick(t, 256),
                    emit_residual=True, norm_dtype=BF16, name="mlstm_outproj")


def _gmlp_layer(h2d, xn, w_in, ln_g, ln_b, w_s, b_s, w_out, final_g):
    t, d = h2d.shape
    inner = ln_g.shape[0]
    ch = w_s.shape[1]
    uv = _proj(xn, w_in, 0, 2 * inner, act="gelu", name="gmlp_proj_uv")
    sz = _proj(xn, w_in, 2 * inner, inner, act="silu", name="gmlp_proj_z")
    y = _gmlp_mix(uv, sz, ln_g, ln_b, w_s, jnp.transpose(b_s), tm=max(ch, _pick(t, 256)))
    (out,) = _outproj(y, w_out.astype(BF16), h2d, final_g, tm=_pick(t, 256),
                      emit_residual=False, norm_dtype=F32, name="gmlp_outproj")
    return out


def kernel(x, mlstm_norm_g, mlstm_w_in, mlstm_conv_w, mlstm_conv_b, mlstm_gate_b, mlstm_head_g,
           mlstm_w_out, gmlp_norm_g, gmlp_w_in, gmlp_ln_g, gmlp_ln_b, gmlp_w_s, gmlp_b_s,
           gmlp_w_out, final_norm_g):
    batch, seq, d = x.shape
    assert mlstm_w_in.shape[0] == 1 and gmlp_w_in.shape[0] == 1, "one mLSTM and one gMLP layer"
    h0 = x.reshape(batch * seq, d)
    h1, xn1 = _mlstm_layer(h0, mlstm_norm_g[0], mlstm_w_in[0], mlstm_conv_w[0], mlstm_conv_b[0],
                           mlstm_gate_b[0], mlstm_head_g[0], mlstm_w_out[0], gmlp_norm_g[0],
                           batch=batch, seq=seq)
    out = _gmlp_layer(h1, xn1, gmlp_w_in[0], gmlp_ln_g[0], gmlp_ln_b[0], gmlp_w_s[0],
                      gmlp_b_s[0], gmlp_w_out[0], final_norm_g)
    return out.reshape(batch, seq, d)
```

```python
import functools
import math

import jax
import jax.numpy as jnp
from jax import lax
from jax.experimental import pallas as pl
from jax.experimental.pallas import tpu as pltpu

EPS = 1e-6
F32 = jnp.float32
BF16 = jnp.bfloat16
LANES = 128
SUBLANES = 8
BF16_ROWS = 16
VMEM_LIMIT_BYTES = 56 * 1024 * 1024
MLSTM_CELL_CHUNK = 256
PROJ_TILE_M = 1024
PROJ_TILE_N = 1024
CONV_TILE_M = 1024
CONV_ROW_SPLIT = 4

_NT = (((1,), (1,)), ((), ()))


def _params(*sem):
    return pltpu.CompilerParams(dimension_semantics=sem, vmem_limit_bytes=VMEM_LIMIT_BYTES)


def _sigmoid(x):
    return 1.0 / (1.0 + jnp.exp(-x))


def _silu(x):
    return x * _sigmoid(x)


def _gelu_exact(x):
    return 0.5 * x * (1.0 + lax.erf(x * math.sqrt(0.5)))


def _log_sigmoid(x):
    return jnp.minimum(x, 0.0) - jnp.log1p(jnp.exp(-jnp.abs(x)))


_ACTIVATIONS = {"none": lambda x: x, "sigmoid": _sigmoid, "silu": _silu, "gelu": _gelu_exact}


def _pick(n, pref):
    tile = min(n, pref)
    while n % tile:
        tile //= 2
    return tile


def _norm_gate_kernel(x_ref, g_ref, wg_ref, gb_ref, xn_ref, cols_ref, rows_ref, *, heads):
    x = x_ref[...]
    xn = (x * lax.rsqrt(jnp.mean(x * x, axis=-1, keepdims=True) + EPS) * g_ref[...]).astype(BF16)
    xn_ref[...] = xn
    L = x.shape[0]
    gates = lax.dot_general(wg_ref[...].astype(BF16), xn, _NT,
                            preferred_element_type=F32) + gb_ref[...]
    ig = gates[0:heads]
    lf = _log_sigmoid(gates[heads:2 * heads])
    r = lax.broadcasted_iota(jnp.int32, (L, L), 0)
    c = lax.broadcasted_iota(jnp.int32, (L, L), 1)
    b = jnp.dot(lf, (r <= c).astype(F32), preferred_element_type=F32,
                precision=lax.Precision.HIGHEST)
    nc = ig - b
    lane = lax.broadcasted_iota(jnp.int32, nc.shape, 1)
    pm = nc
    shift = 1
    while shift < L:
        pm = jnp.maximum(pm, jnp.where(lane >= shift, pltpu.roll(pm, shift, axis=1), -jnp.inf))
        shift *= 2
    rows_ref[...] = nc
    pad = jnp.zeros((LANES - 2 * heads, L), F32)
    cols_ref[...] = jnp.concatenate([b, pm, pad], axis=0).T


def _norm_gates(x, g, w_t, gate_b, *, chunk, gate_row0):
    t, d = x.shape
    heads = gate_b.shape[0] // 2
    assert heads % SUBLANES == 0 and gate_row0 % (2 * heads) == 0
    row = pl.BlockSpec((chunk, d), lambda i: (i, 0))
    return pl.pallas_call(
        functools.partial(_norm_gate_kernel, heads=heads),
        grid=(t // chunk,),
        in_specs=[row, pl.BlockSpec((1, d), lambda i: (0, 0)),
                  pl.BlockSpec((2 * heads, d), lambda i: (gate_row0 // (2 * heads), 0)),
                  pl.BlockSpec((2 * heads, 1), lambda i: (0, 0))],
        out_specs=[row, pl.BlockSpec((chunk, LANES), lambda i: (i, 0)),
                   pl.BlockSpec((heads, chunk), lambda i: (0, i))],
        out_shape=[jax.ShapeDtypeStruct((t, d), BF16),
                   jax.ShapeDtypeStruct((t, LANES), F32),
                   jax.ShapeDtypeStruct((heads, t), F32)],
        compiler_params=_params("arbitrary"),
        name="norm_gates",
    )(x, g.reshape(1, d), w_t, gate_b.reshape(2 * heads, 1))


def _weight_spec(k, tn, jb, w_is_nk):
    if w_is_nk:
        return pl.BlockSpec((tn, k), lambda j, i: (jb + j, 0)), (tn, k)
    return pl.BlockSpec((k, tn), lambda j, i: (0, jb + j)), (k, tn)


def _mm(x, w, w_is_nk):
    if w_is_nk:
        return lax.dot_general(x, w, _NT, preferred_element_type=F32)
    return jnp.dot(x, w, preferred_element_type=F32)


def _proj_kernel(x_ref, w_ref, o_ref, wbf_ref, *, act, w_is_nk):
    @pl.when(pl.program_id(1) == 0)
    def _():
        wbf_ref[...] = w_ref[...].astype(BF16)

    acc = _mm(x_ref[...], wbf_ref[...], w_is_nk)
    o_ref[...] = _ACTIVATIONS[act](acc).astype(o_ref.dtype)


def _proj(x, w, col0, n_out, *, act, w_is_nk, name):
    t, k = x.shape
    tm, tn = _pick(t, PROJ_TILE_M), _pick(n_out, PROJ_TILE_N)
    assert col0 % tn == 0
    w_spec, w_block = _weight_spec(k, tn, col0 // tn, w_is_nk)
    return pl.pallas_call(
        functools.partial(_proj_kernel, act=act, w_is_nk=w_is_nk),
        grid=(n_out // tn, t // tm),
        in_specs=[pl.BlockSpec((tm, k), lambda j, i: (i, 0)), w_spec],
        out_specs=pl.BlockSpec((tm, tn), lambda j, i: (i, j)),
        out_shape=jax.ShapeDtypeStruct((t, n_out), BF16),
        scratch_shapes=[pltpu.VMEM(w_block, BF16)],
        compiler_params=_params("arbitrary", "arbitrary"),
        name=name,
    )(x, w)


def _proj_conv_kernel(x_ref, xh_ref, w_ref, cw_ref, cb_ref, o_ref, wbf_ref,
                      *, seq, q_tiles, q_scale, w_is_nk):
    j, i = pl.program_id(0), pl.program_id(1)
    tm = x_ref.shape[0]
    conv_k = cw_ref.shape[0]

    @pl.when(i == 0)
    def _():
        wbf_ref[...] = w_ref[...].astype(BF16)

    w = wbf_ref[...]
    scale = jnp.where(j < q_tiles, q_scale, 1.0).astype(F32)
    prev = _mm(xh_ref[...], w, w_is_nk)[-SUBLANES:, :]
    prev = jnp.where((i * tm) % seq == 0, 0.0, prev)
    sub = lax.broadcasted_iota(jnp.int32, prev.shape, 0)
    rb = tm // CONV_ROW_SPLIT
    for mb in range(CONV_ROW_SPLIT):
        rows = slice(mb * rb, (mb + 1) * rb)
        p = _mm(x_ref[rows, :], w, w_is_nk)
        acc = cb_ref[...] + cw_ref[conv_k - 1:conv_k, :] * p
        for d in range(1, conv_k):
            rolled = pltpu.roll(p, d, axis=0)
            head = jnp.where(sub < d, pltpu.roll(prev, d, axis=0), rolled[0:SUBLANES, :])
            shifted = jnp.concatenate([head, rolled[SUBLANES:, :]], axis=0)
            acc = acc + cw_ref[conv_k - 1 - d:conv_k - d, :] * shifted
        o_ref[rows, :] = (_silu(acc) * scale).astype(o_ref.dtype)
        prev = p[-SUBLANES:, :]


def _proj_conv(x, w, conv_w, conv_b, *, seq, q_scale, w_is_nk, name):
    t, k = x.shape
    conv_k, n_out = conv_w.shape
    assert conv_k - 1 <= SUBLANES
    tm, tn = _pick(seq, CONV_TILE_M), _pick(n_out // 2, PROJ_TILE_N)
    hb = tm // BF16_ROWS
    w_spec, w_block = _weight_spec(k, tn, 0, w_is_nk)
    kern = functools.partial(_proj_conv_kernel, seq=seq, q_tiles=n_out // 2 // tn,
                             q_scale=q_scale, w_is_nk=w_is_nk)
    return pl.pallas_call(
        kern,
        grid=(n_out // tn, t // tm),
        in_specs=[pl.BlockSpec((tm, k), lambda j, i: (i, 0)),
                  pl.BlockSpec((BF16_ROWS, k), lambda j, i: (jnp.maximum(i * hb - 1, 0), 0)),
                  w_spec,
                  pl.BlockSpec((conv_k, tn), lambda j, i: (0, j)),
                  pl.BlockSpec((1, tn), lambda j, i: (0, j))],
        out_specs=pl.BlockSpec((tm, tn), lambda j, i: (i, j)),
        out_shape=jax.ShapeDtypeStruct((t, n_out), BF16),
        scratch_shapes=[pltpu.VMEM(w_block, BF16)],
        compiler_params=_params("arbitrary", "arbitrary"),
        name=name,
    )(x, x, w, conv_w, conv_b.reshape(1, n_out))


def _mlstm_cell_kernel(qk_ref, v_ref, so_ref, sz_ref, cols_ref, rows_ref, hg_ref, y_ref,
                       c_ref, n_ref, m_ref, *, heads, dk, dv):
    L = qk_ref.shape[0]
    qk_w = heads * dk

    @pl.when(pl.program_id(1) == 0)
    def _():
        c_ref[...] = jnp.zeros(c_ref.shape, F32)
        n_ref[...] = jnp.zeros(n_ref.shape, F32)
        m_ref[...] = jnp.zeros(m_ref.shape, F32)

    r = lax.broadcasted_iota(jnp.int32, (L, L), 0)
    c = lax.broadcasted_iota(jnp.int32, (L, L), 1)
    causal = r >= c
    ones = jnp.ones((SUBLANES, L), BF16)

    for h in range(heads):
        vcols = slice(h * dv, (h + 1) * dv)
        q = qk_ref[:, h * dk:(h + 1) * dk]
        k = qk_ref[:, qk_w + h * dk:qk_w + (h + 1) * dk]
        v = v_ref[:, vcols]

        bcol = cols_ref[:, h:h + 1]
        pmcol = cols_ref[:, heads + h:heads + h + 1]
        ncrow = rows_ref[h:h + 1, :]
        m = m_ref[h:h + 1, 0:1]
        g = bcol[L - 1:L, :]

        u = jnp.maximum(m, pmcol)
        dexp = jnp.exp(jnp.where(causal, ncrow - u, -jnp.inf))
        s = lax.dot_general(q, k, _NT, preferred_element_type=F32) * dexp
        a = jnp.exp(m - u)
        c_st = c_ref[h]
        num = (a * jnp.dot(q, c_st.astype(BF16), preferred_element_type=F32)
               + jnp.dot(s.astype(BF16), v, preferred_element_type=F32))
        qn_terms = q.astype(F32) * n_ref[h:h + 1, :]
        qn = jnp.sum(qn_terms[:, :dk // 2] + qn_terms[:, dk // 2:], axis=-1, keepdims=True)
        den = a * qn + jnp.sum(s, axis=-1, keepdims=True)
        hc = num * (1.0 / jnp.maximum(jnp.abs(den), jnp.exp(-(bcol + u))))

        m_new = g + jnp.maximum(m, pmcol[L - 1:L, :])
        wexp = jnp.exp((g + ncrow) - m_new)
        kw_t = (k.T.astype(F32) * wexp).astype(BF16)
        decay = jnp.exp(g + m - m_new)
        c_ref[h] = decay * c_st + jnp.dot(kw_t, v, preferred_element_type=F32)
        n_add = lax.dot_general(ones, kw_t, _NT, preferred_element_type=F32)[0:1, :]
        n_ref[h:h + 1, :] = decay * n_ref[h:h + 1, :] + n_add
        m_ref[h:h + 1, :] = jnp.broadcast_to(m_new, (1, m_ref.shape[1]))

        hc = hc * so_ref[:, vcols].astype(F32)
        hc = hc * lax.rsqrt(jnp.mean(hc * hc, axis=-1, keepdims=True) + EPS)
        hc = hc * hg_ref[:, vcols]
        y_ref[:, vcols] = (hc * sz_ref[:, vcols].astype(F32)).astype(y_ref.dtype)


def _mlstm_cell(qk, v, so, sz, cols, rows, head_g, *, batch, seq, heads, chunk):
    t, inner = v.shape
    qk_w = qk.shape[1] // 2
    dk, dv = qk_w // heads, inner // heads
    nc = seq // chunk
    row = lambda b, c: b * nc + c
    wide = lambda width: pl.BlockSpec((chunk, width), lambda b, c: (row(b, c), 0))
    kern = functools.partial(_mlstm_cell_kernel, heads=heads, dk=dk, dv=dv)
    return pl.pallas_call(
        kern,
        grid=(batch, nc),
        in_specs=[wide(2 * qk_w), wide(inner), wide(inner), wide(inner), wide(LANES),
                  pl.BlockSpec((heads, chunk), lambda b, c: (0, row(b, c))),
                  pl.BlockSpec((1, inner), lambda b, c: (0, 0))],
        out_specs=wide(inner),
        out_shape=jax.ShapeDtypeStruct((t, inner), BF16),
        scratch_shapes=[pltpu.VMEM((heads, dk, dv), F32),
                        pltpu.VMEM((heads, dk), F32),
                        pltpu.VMEM((heads, LANES), F32)],
        compiler_params=_params("arbitrary", "arbitrary"),
        name="mlstm_cell",
    )(qk, v, so, sz, cols, rows, head_g.reshape(1, inner))


def _outproj_kernel(y_ref, w_ref, r_ref, g_ref, *out_refs, emit_residual):
    h = r_ref[...] + jnp.dot(y_ref[...], w_ref[...], preferred_element_type=F32)
    xn = h * lax.rsqrt(jnp.mean(h * h, axis=-1, keepdims=True) + EPS) * g_ref[...]
    if emit_residual:
        h_ref, xn_ref = out_refs
        h_ref[...] = h
    else:
        (xn_ref,) = out_refs
    xn_ref[...] = xn.astype(xn_ref.dtype)


def _outproj(y, w, resid, g, *, tm, emit_residual, norm_dtype, name):
    t, k = y.shape
    d = w.shape[1]
    row = lambda width: pl.BlockSpec((tm, width), lambda i: (i, 0))
    out_specs = [row(d)]
    out_shape = [jax.ShapeDtypeStruct((t, d), norm_dtype)]
    if emit_residual:
        out_specs = [row(d)] + out_specs
        out_shape = [jax.ShapeDtypeStruct((t, d), F32)] + out_shape
    return pl.pallas_call(
        functools.partial(_outproj_kernel, emit_residual=emit_residual),
        grid=(t // tm,),
        in_specs=[row(k), pl.BlockSpec((k, d), lambda i: (0, 0)), row(d),
                  pl.BlockSpec((1, d), lambda i: (0, 0))],
        out_specs=out_specs,
        out_shape=out_shape,
        compiler_params=_params("arbitrary"),
        name=name,
    )(y, w, resid, g.reshape(1, d))


def _gmlp_mix_kernel(u_ref, v_ref, sz_ref, lg_ref, lb_ref, ws_ref, bs_ref, y_ref, vn_ref,
                     *, groups):
    tm, inner = v_ref.shape
    ch = ws_ref.shape[1]
    gd = inner // groups
    v = v_ref[...].astype(F32)
    mu = jnp.mean(v, axis=-1, keepdims=True)
    d = v - mu
    var = jnp.mean(d * d, axis=-1, keepdims=True)
    vn_ref[...] = (d * lax.rsqrt(var + EPS) * lg_ref[...] + lb_ref[...]).astype(vn_ref.dtype)

    r = lax.broadcasted_iota(jnp.int32, (ch, ch), 0)
    c = lax.broadcasted_iota(jnp.int32, (ch, ch), 1)
    mask = (r >= c).astype(F32)
    for g in range(groups):
        ws = (ws_ref[g] * mask).astype(BF16)
        cols = slice(g * gd, (g + 1) * gd)
        for k in range(tm // ch):
            rows = slice(k * ch, (k + 1) * ch)
            sv = jnp.dot(ws, vn_ref[rows, cols], preferred_element_type=F32) + bs_ref[:, g:g + 1]
            u = u_ref[rows, cols].astype(F32)
            y_ref[rows, cols] = (u * sv * sz_ref[rows, cols].astype(F32)).astype(y_ref.dtype)


def _gmlp_mix(uv, sz, ln_g, ln_b, w_s, b_s_t, *, tm):
    t, inner = sz.shape
    groups, ch, _ = w_s.shape
    wide = lambda j: pl.BlockSpec((tm, inner), lambda i: (i, j))
    full = lambda shape: pl.BlockSpec(shape, lambda i: (0,) * len(shape))
    return pl.pallas_call(
        functools.partial(_gmlp_mix_kernel, groups=groups),
        grid=(t // tm,),
        in_specs=[wide(0), wide(1), wide(0), full((1, inner)), full((1, inner)),
                  full((groups, ch, ch)), full((ch, groups))],
        out_specs=pl.BlockSpec((tm, inner), lambda i: (i, 0)),
        out_shape=jax.ShapeDtypeStruct((t, inner), BF16),
        scratch_shapes=[pltpu.VMEM((tm, inner), BF16)],
        compiler_params=_params("arbitrary"),
        name="gmlp_mix",
    )(uv, uv, sz, ln_g.reshape(1, inner), ln_b.reshape(1, inner), w_s, b_s_t)


def _mlstm_layer(h2d, norm_g, w_in, conv_w, conv_b, gate_b, head_g, w_out, next_g,
                 *, batch, seq):
    t, d = h2d.shape
    heads = gate_b.shape[0] // 2
    qk_w = conv_w.shape[1] // 2
    inner = head_g.shape[0]
    dk = qk_w // heads
    n_main = 2 * qk_w + 3 * inner
    chunk = _pick(seq, MLSTM_CELL_CHUNK)

    w_t = jnp.transpose(w_in)
    xn, cols, rows = _norm_gates(h2d, norm_g, w_t, gate_b, chunk=chunk, gate_row0=n_main)

    proj = functools.partial(_proj, xn, w_t, w_is_nk=True)
    qk = _proj_conv(xn, w_t, conv_w, conv_b, seq=seq, q_scale=dk ** -0.5, w_is_nk=True,
                    name="mlstm_proj_qk")
    v = proj(2 * qk_w, inner, act="none", name="mlstm_proj_v")
    so = proj(2 * qk_w + inner, inner, act="sigmoid", name="mlstm_proj_o")
    sz = proj(2 * qk_w + 2 * inner, inner, act="silu", name="mlstm_proj_z")

    y = _mlstm_cell(qk, v, so, sz, cols, rows, head_g, batch=batch, seq=seq, heads=heads,
                    chunk=chunk)
    return _outproj(y, w_out.astype(BF16), h2d, next_g, tm=_pick(t, 256),
                    emit_residual=True, norm_dtype=BF16, name="mlstm_outproj")


def _gmlp_layer(h2d, xn, w_in, ln_g, ln_b, w_s, b_s, w_out, final_g):
    t, d = h2d.shape
    inner = ln_g.shape[0]
    ch = w_s.shape[1]
    proj = functools.partial(_proj, xn, w_in, w_is_nk=False)
    uv = proj(0, 2 * inner, act="gelu", name="gmlp_proj_uv")
    sz = proj(2 * inner, inner, act="silu", name="gmlp_proj_z")
    y = _gmlp_mix(uv, sz, ln_g, ln_b, w_s, jnp.transpose(b_s), tm=max(ch, _pick(t, 256)))
    (out,) = _outproj(y, w_out.astype(BF16), h2d, final_g, tm=_pick(t, 256),
                      emit_residual=False, norm_dtype=F32, name="gmlp_outproj")
    return out


def kernel(x, mlstm_norm_g, mlstm_w_in, mlstm_conv_w, mlstm_conv_b, mlstm_gate_b, mlstm_head_g,
           mlstm_w_out, gmlp_norm_g, gmlp_w_in, gmlp_ln_g, gmlp_ln_b, gmlp_w_s, gmlp_b_s,
           gmlp_w_out, final_norm_g):
    batch, seq, d = x.shape
    assert mlstm_w_in.shape[0] == 1 and gmlp_w_in.shape[0] == 1, "one mLSTM and one gMLP layer"
    h0 = x.reshape(batch * seq, d)
    h1, xn1 = _mlstm_layer(h0, mlstm_norm_g[0], mlstm_w_in[0], mlstm_conv_w[0], mlstm_conv_b[0],
                           mlstm_gate_b[0], mlstm_head_g[0], mlstm_w_out[0], gmlp_norm_g[0],
                           batch=batch, seq=seq)
    out = _gmlp_layer(h1, xn1, gmlp_w_in[0], gmlp_ln_g[0], gmlp_ln_b[0], gmlp_w_s[0],
                      gmlp_b_s[0], gmlp_w_out[0], final_norm_g)
    return out.reshape(batch, seq, d)
```

```python
import functools
import math

import jax
import jax.numpy as jnp
from jax import lax
from jax.experimental import pallas as pl
from jax.experimental.pallas import tpu as pltpu

EPS = 1e-6
F32 = jnp.float32
BF16 = jnp.bfloat16
LANES = 128
SUBLANES = 8
BF16_ROWS = 16
VMEM_LIMIT_BYTES = 56 * 1024 * 1024
MLSTM_CELL_CHUNK = 256
PROJ_TILE_M = 2048
PROJ_TILE_N = 1024
PROJ_SUB_ROWS = 256

_NT = (((1,), (1,)), ((), ()))


def _params(*sem):
    return pltpu.CompilerParams(dimension_semantics=sem, vmem_limit_bytes=VMEM_LIMIT_BYTES)


def _sigmoid(x):
    return 1.0 / (1.0 + jnp.exp(-x))


def _silu(x):
    return x * _sigmoid(x)


def _gelu_exact(x):
    return 0.5 * x * (1.0 + lax.erf(x * math.sqrt(0.5)))


def _log_sigmoid(x):
    return jnp.minimum(x, 0.0) - jnp.log1p(jnp.exp(-jnp.abs(x)))


_ACTIVATIONS = {"none": lambda x: x, "sigmoid": _sigmoid, "silu": _silu, "gelu": _gelu_exact}


def _pick(n, pref):
    tile = min(n, pref)
    while n % tile:
        tile //= 2
    return tile


def _norm_gate_kernel(x_ref, g_ref, wg_ref, gb_ref, xn_ref, rows_ref, *, heads):
    x = x_ref[...]
    xn = (x * lax.rsqrt(jnp.mean(x * x, axis=-1, keepdims=True) + EPS) * g_ref[...]).astype(BF16)
    xn_ref[...] = xn
    L = x.shape[0]
    gates = lax.dot_general(wg_ref[...].astype(BF16), xn, _NT,
                            preferred_element_type=F32) + gb_ref[...]
    ig = gates[0:heads]
    lf = _log_sigmoid(gates[heads:2 * heads])
    r = lax.broadcasted_iota(jnp.int32, (L, L), 0)
    c = lax.broadcasted_iota(jnp.int32, (L, L), 1)
    b = jnp.dot(lf, (r <= c).astype(F32), preferred_element_type=F32,
                precision=lax.Precision.HIGHEST)
    nc = ig - b
    lane = lax.broadcasted_iota(jnp.int32, nc.shape, 1)
    pm = nc
    shift = 1
    while shift < L:
        pm = jnp.maximum(pm, jnp.where(lane >= shift, pltpu.roll(pm, shift, axis=1), -jnp.inf))
        shift *= 2
    rows_ref[...] = jnp.concatenate([nc, b, pm], axis=0)


def _norm_gates(x, g, w_t, gate_b, *, chunk, gate_row0):
    t, d = x.shape
    heads = gate_b.shape[0] // 2
    assert heads % SUBLANES == 0 and gate_row0 % (2 * heads) == 0
    row = pl.BlockSpec((chunk, d), lambda i: (i, 0))
    return pl.pallas_call(
        functools.partial(_norm_gate_kernel, heads=heads),
        grid=(t // chunk,),
        in_specs=[row, pl.BlockSpec((1, d), lambda i: (0, 0)),
                  pl.BlockSpec((2 * heads, d), lambda i: (gate_row0 // (2 * heads), 0)),
                  pl.BlockSpec((2 * heads, 1), lambda i: (0, 0))],
        out_specs=[row, pl.BlockSpec((3 * heads, chunk), lambda i: (0, i))],
        out_shape=[jax.ShapeDtypeStruct((t, d), BF16),
                   jax.ShapeDtypeStruct((3 * heads, t), F32)],
        compiler_params=_params("arbitrary"),
        name="norm_gates",
    )(x, g.reshape(1, d), w_t, gate_b.reshape(2 * heads, 1))


def _weight_spec(k, tn, jb, w_is_nk):
    if w_is_nk:
        return pl.BlockSpec((tn, k), lambda j, i: (jb + j, 0)), (tn, k)
    return pl.BlockSpec((k, tn), lambda j, i: (0, jb + j)), (k, tn)


def _mm(x, w, w_is_nk):
    if w_is_nk:
        return lax.dot_general(x, w, _NT, preferred_element_type=F32)
    return jnp.dot(x, w, preferred_element_type=F32)


def _proj_kernel(x_ref, w_ref, *rest, act, w_is_nk, has_gain):
    gain_ref = rest[0] if has_gain else None
    o_ref, wbf_ref = rest[-2:]

    @pl.when(pl.program_id(1) == 0)
    def _():
        wbf_ref[...] = w_ref[...].astype(BF16)

    w = wbf_ref[...]
    for r0 in range(0, x_ref.shape[0], PROJ_SUB_ROWS):
        rows = slice(r0, r0 + PROJ_SUB_ROWS)
        out = _ACTIVATIONS[act](_mm(x_ref[rows, :], w, w_is_nk))
        if has_gain:
            out = out * gain_ref[...]
        o_ref[rows, :] = out.astype(o_ref.dtype)


def _proj(x, w, col0, n_out, *, act, w_is_nk, name, gain=None):
    t, k = x.shape
    tm, tn = _pick(t, PROJ_TILE_M), _pick(n_out, PROJ_TILE_N)
    assert col0 % tn == 0
    w_spec, w_block = _weight_spec(k, tn, col0 // tn, w_is_nk)
    in_specs = [pl.BlockSpec((tm, k), lambda j, i: (i, 0)), w_spec]
    operands = [x, w]
    if gain is not None:
        in_specs.append(pl.BlockSpec((1, tn), lambda j, i: (0, j)))
        operands.append(gain.reshape(1, n_out))
    return pl.pallas_call(
        functools.partial(_proj_kernel, act=act, w_is_nk=w_is_nk, has_gain=gain is not None),
        grid=(n_out // tn, t // tm),
        in_specs=in_specs,
        out_specs=pl.BlockSpec((tm, tn), lambda j, i: (i, j)),
        out_shape=jax.ShapeDtypeStruct((t, n_out), BF16),
        scratch_shapes=[pltpu.VMEM(w_block, BF16)],
        compiler_params=_params("arbitrary", "arbitrary"),
        name=name,
    )(*operands)


def _proj_conv_kernel(x_ref, xh_ref, w_ref, cw_ref, cb_ref, o_ref, wbf_ref,
                      *, seq, q_tiles, q_scale, w_is_nk):
    j, i = pl.program_id(0), pl.program_id(1)
    tm = x_ref.shape[0]
    conv_k = cw_ref.shape[0]

    @pl.when(i == 0)
    def _():
        wbf_ref[...] = w_ref[...].astype(BF16)

    w = wbf_ref[...]
    scale = jnp.where(j < q_tiles, q_scale, 1.0).astype(F32)
    prev = _mm(xh_ref[...], w, w_is_nk)[-SUBLANES:, :]
    prev = jnp.where((i * tm) % seq == 0, 0.0, prev)
    sub = lax.broadcasted_iota(jnp.int32, prev.shape, 0)
    for r0 in range(0, tm, PROJ_SUB_ROWS):
        rows = slice(r0, r0 + PROJ_SUB_ROWS)
        p = _mm(x_ref[rows, :], w, w_is_nk)
        acc = cb_ref[...] + cw_ref[conv_k - 1:conv_k, :] * p
        for d in range(1, conv_k):
            rolled = pltpu.roll(p, d, axis=0)
            head = jnp.where(sub < d, pltpu.roll(prev, d, axis=0), rolled[0:SUBLANES, :])
            shifted = jnp.concatenate([head, rolled[SUBLANES:, :]], axis=0)
            acc = acc + cw_ref[conv_k - 1 - d:conv_k - d, :] * shifted
        o_ref[rows, :] = (_silu(acc) * scale).astype(o_ref.dtype)
        prev = p[-SUBLANES:, :]


def _proj_conv(x, w, conv_w, conv_b, *, seq, q_scale, w_is_nk, name):
    t, k = x.shape
    conv_k, n_out = conv_w.shape
    assert conv_k - 1 <= SUBLANES
    tm, tn = _pick(seq, PROJ_TILE_M), _pick(n_out // 2, PROJ_TILE_N)
    hb = tm // BF16_ROWS
    w_spec, w_block = _weight_spec(k, tn, 0, w_is_nk)
    kern = functools.partial(_proj_conv_kernel, seq=seq, q_tiles=n_out // 2 // tn,
                             q_scale=q_scale, w_is_nk=w_is_nk)
    return pl.pallas_call(
        kern,
        grid=(n_out // tn, t // tm),
        in_specs=[pl.BlockSpec((tm, k), lambda j, i: (i, 0)),
                  pl.BlockSpec((BF16_ROWS, k), lambda j, i: (jnp.maximum(i * hb - 1, 0), 0)),
                  w_spec,
                  pl.BlockSpec((conv_k, tn), lambda j, i: (0, j)),
                  pl.BlockSpec((1, tn), lambda j, i: (0, j))],
        out_specs=pl.BlockSpec((tm, tn), lambda j, i: (i, j)),
        out_shape=jax.ShapeDtypeStruct((t, n_out), BF16),
        scratch_shapes=[pltpu.VMEM(w_block, BF16)],
        compiler_params=_params("arbitrary", "arbitrary"),
        name=name,
    )(x, x, w, conv_w, conv_b.reshape(1, n_out))


def _mlstm_cell_kernel(qk_ref, v_ref, so_ref, szg_ref, rows_ref, y_ref, c_ref, n_ref, m_ref,
                       *, heads, dk, dv):
    L = qk_ref.shape[0]
    qk_w = heads * dk

    @pl.when(pl.program_id(1) == 0)
    def _():
        c_ref[...] = jnp.zeros(c_ref.shape, F32)
        n_ref[...] = jnp.zeros(n_ref.shape, F32)
        m_ref[...] = jnp.zeros(m_ref.shape, F32)

    r = lax.broadcasted_iota(jnp.int32, (L, L), 0)
    c = lax.broadcasted_iota(jnp.int32, (L, L), 1)
    causal = r >= c
    ones_l = jnp.ones((SUBLANES, L), BF16)
    ones_v = jnp.ones((SUBLANES, dv), BF16)

    def as_rows(row, width):
        col = jnp.broadcast_to(row, (LANES, L)).T
        return jnp.concatenate([col] * (width // LANES), axis=1)

    def row_sum(ones, mat):
        return lax.dot_general(ones, mat, _NT, preferred_element_type=F32)[0:1, :]

    for h in range(heads):
        vcols = slice(h * dv, (h + 1) * dv)
        q = qk_ref[:, h * dk:(h + 1) * dk]
        k = qk_ref[:, qk_w + h * dk:qk_w + (h + 1) * dk]
        v = v_ref[:, vcols]
        nc = rows_ref[h:h + 1, :]
        b = rows_ref[heads + h:heads + h + 1, :]
        pm = rows_ref[2 * heads + h:2 * heads + h + 1, :]
        m = m_ref[h:h + 1, 0:1]

        u = jnp.maximum(m, pm)
        a = jnp.exp(m - u)
        dexp = jnp.exp(jnp.where(causal, nc - as_rows(u, L), -jnp.inf))
        s = (lax.dot_general(q, k, _NT, preferred_element_type=F32) * dexp).astype(BF16)
        num = (as_rows(a, dv) * jnp.dot(q, c_ref[h].astype(BF16), preferred_element_type=F32)
               + jnp.dot(s, v, preferred_element_type=F32))
        n_rows = jnp.broadcast_to(n_ref[h:h + 1, :], (SUBLANES, dk)).astype(BF16)
        den = a * row_sum(n_rows, q) + row_sum(ones_l, s)
        rec = 1.0 / jnp.maximum(jnp.abs(den), jnp.exp(-(b + u)))

        t = num * so_ref[:, vcols].astype(F32)
        rms = rec * jnp.sqrt(row_sum(ones_v, (t * t).astype(BF16)) * (1.0 / dv))
        scale = rec * lax.rsqrt(rms * rms + EPS)
        y_ref[:, vcols] = (t * as_rows(scale, dv)
                           * szg_ref[:, vcols].astype(F32)).astype(y_ref.dtype)

        g = b[:, L - 1:L]
        m_new = g + jnp.maximum(m, pm[:, L - 1:L])
        kw_t = (k.T.astype(F32) * jnp.exp((g + nc) - m_new)).astype(BF16)
        decay = jnp.exp(g + m - m_new)
        c_ref[h] = decay * c_ref[h] + jnp.dot(kw_t, v, preferred_element_type=F32)
        n_ref[h:h + 1, :] = decay * n_ref[h:h + 1, :] + row_sum(ones_l, kw_t)
        m_ref[h:h + 1, :] = jnp.broadcast_to(m_new, (1, m_ref.shape[1]))


def _mlstm_cell(qk, v, so, szg, rows, *, batch, seq, heads, chunk):
    t, inner = v.shape
    qk_w = qk.shape[1] // 2
    dk, dv = qk_w // heads, inner // heads
    nc = seq // chunk
    row = lambda b, c: b * nc + c
    wide = lambda width: pl.BlockSpec((chunk, width), lambda b, c: (row(b, c), 0))
    kern = functools.partial(_mlstm_cell_kernel, heads=heads, dk=dk, dv=dv)
    return pl.pallas_call(
        kern,
        grid=(batch, nc),
        in_specs=[wide(2 * qk_w), wide(inner), wide(inner), wide(inner),
                  pl.BlockSpec((3 * heads, chunk), lambda b, c: (0, row(b, c)))],
        out_specs=wide(inner),
        out_shape=jax.ShapeDtypeStruct((t, inner), BF16),
        scratch_shapes=[pltpu.VMEM((heads, dk, dv), F32),
                        pltpu.VMEM((heads, dk), F32),
                        pltpu.VMEM((heads, LANES), F32)],
        compiler_params=_params("arbitrary", "arbitrary"),
        name="mlstm_cell",
    )(qk, v, so, szg, rows)


def _outproj_kernel(y_ref, w_ref, r_ref, g_ref, *out_refs, emit_residual):
    h = r_ref[...] + jnp.dot(y_ref[...], w_ref[...], preferred_element_type=F32)
    xn = h * lax.rsqrt(jnp.mean(h * h, axis=-1, keepdims=True) + EPS) * g_ref[...]
    if emit_residual:
        h_ref, xn_ref = out_refs
        h_ref[...] = h
    else:
        (xn_ref,) = out_refs
    xn_ref[...] = xn.astype(xn_ref.dtype)


def _outproj(y, w, resid, g, *, tm, emit_residual, norm_dtype, name):
    t, k = y.shape
    d = w.shape[1]
    row = lambda width: pl.BlockSpec((tm, width), lambda i: (i, 0))
    out_specs = [row(d)]
    out_shape = [jax.ShapeDtypeStruct((t, d), norm_dtype)]
    if emit_residual:
        out_specs = [row(d)] + out_specs
        out_shape = [jax.ShapeDtypeStruct((t, d), F32)] + out_shape
    return pl.pallas_call(
        functools.partial(_outproj_kernel, emit_residual=emit_residual),
        grid=(t // tm,),
        in_specs=[row(k), pl.BlockSpec((k, d), lambda i: (0, 0)), row(d),
                  pl.BlockSpec((1, d), lambda i: (0, 0))],
        out_specs=out_specs,
        out_shape=out_shape,
        compiler_params=_params("arbitrary"),
        name=name,
    )(y, w, resid, g.reshape(1, d))


def _gmlp_mix_kernel(u_ref, v_ref, sz_ref, lg_ref, lb_ref, ws_ref, bs_ref, y_ref, vn_ref,
                     *, groups):
    tm, inner = v_ref.shape
    ch = ws_ref.shape[1]
    gd = inner // groups
    v = v_ref[...].astype(F32)
    mu = jnp.mean(v, axis=-1, keepdims=True)
    d = v - mu
    var = jnp.mean(d * d, axis=-1, keepdims=True)
    vn_ref[...] = (d * lax.rsqrt(var + EPS) * lg_ref[...] + lb_ref[...]).astype(vn_ref.dtype)

    r = lax.broadcasted_iota(jnp.int32, (ch, ch), 0)
    c = lax.broadcasted_iota(jnp.int32, (ch, ch), 1)
    mask = (r >= c).astype(F32)
    for g in range(groups):
        ws = (ws_ref[g] * mask).astype(BF16)
        cols = slice(g * gd, (g + 1) * gd)
        for k in range(tm // ch):
            rows = slice(k * ch, (k + 1) * ch)
            sv = jnp.dot(ws, vn_ref[rows, cols], preferred_element_type=F32) + bs_ref[:, g:g + 1]
            u = u_ref[rows, cols].astype(F32)
            y_ref[rows, cols] = (u * sv * sz_ref[rows, cols].astype(F32)).astype(y_ref.dtype)


def _gmlp_mix(uv, sz, ln_g, ln_b, w_s, b_s_t, *, tm):
    t, inner = sz.shape
    groups, ch, _ = w_s.shape
    wide = lambda j: pl.BlockSpec((tm, inner), lambda i: (i, j))
    full = lambda shape: pl.BlockSpec(shape, lambda i: (0,) * len(shape))
    return pl.pallas_call(
        functools.partial(_gmlp_mix_kernel, groups=groups),
        grid=(t // tm,),
        in_specs=[wide(0), wide(1), wide(0), full((1, inner)), full((1, inner)),
                  full((groups, ch, ch)), full((ch, groups))],
        out_specs=pl.BlockSpec((tm, inner), lambda i: (i, 0)),
        out_shape=jax.ShapeDtypeStruct((t, inner), BF16),
        scratch_shapes=[pltpu.VMEM((tm, inner), BF16)],
        compiler_params=_params("arbitrary"),
        name="gmlp_mix",
    )(uv, uv, sz, ln_g.reshape(1, inner), ln_b.reshape(1, inner), w_s, b_s_t)


def _mlstm_layer(h2d, norm_g, w_in, conv_w, conv_b, gate_b, head_g, w_out, next_g,
                 *, batch, seq):
    t, d = h2d.shape
    heads = gate_b.shape[0] // 2
    qk_w = conv_w.shape[1] // 2
    inner = head_g.shape[0]
    dk = qk_w // heads
    n_main = 2 * qk_w + 3 * inner
    chunk = _pick(seq, MLSTM_CELL_CHUNK)

    w_t = jnp.transpose(w_in)
    xn, rows = _norm_gates(h2d, norm_g, w_t, gate_b, chunk=chunk, gate_row0=n_main)

    proj = functools.partial(_proj, xn, w_t, w_is_nk=True)
    qk = _proj_conv(xn, w_t, conv_w, conv_b, seq=seq, q_scale=dk ** -0.5, w_is_nk=True,
                    name="mlstm_proj_qk")
    v = proj(2 * qk_w, inner, act="none", name="mlstm_proj_v")
    so = proj(2 * qk_w + inner, inner, act="sigmoid", name="mlstm_proj_o")
    szg = proj(2 * qk_w + 2 * inner, inner, act="silu", gain=head_g, name="mlstm_proj_z")

    y = _mlstm_cell(qk, v, so, szg, rows, batch=batch, seq=seq, heads=heads, chunk=chunk)
    return _outproj(y, w_out.astype(BF16), h2d, next_g, tm=_pick(t, 256),
                    emit_residual=True, norm_dtype=BF16, name="mlstm_outproj")


def _gmlp_layer(h2d, xn, w_in, ln_g, ln_b, w_s, b_s, w_out, final_g):
    t, d = h2d.shape
    inner = ln_g.shape[0]
    ch = w_s.shape[1]
    proj = functools.partial(_proj, xn, w_in, w_is_nk=False)
    uv = proj(0, 2 * inner, act="gelu", name="gmlp_proj_uv")
    sz = proj(2 * inner, inner, act="silu", name="gmlp_proj_z")
    y = _gmlp_mix(uv, sz, ln_g, ln_b, w_s, jnp.transpose(b_s), tm=max(ch, _pick(t, 256)))
    (out,) = _outproj(y, w_out.astype(BF16), h2d, final_g, tm=_pick(t, 256),
                      emit_residual=False, norm_dtype=F32, name="gmlp_outproj")
    return out


def kernel(x, mlstm_norm_g, mlstm_w_in, mlstm_conv_w, mlstm_conv_b, mlstm_gate_b, mlstm_head_g,
           mlstm_w_out, gmlp_norm_g, gmlp_w_in, gmlp_ln_g, gmlp_ln_b, gmlp_w_s, gmlp_b_s,
           gmlp_w_out, final_norm_g):
    batch, seq, d = x.shape
    assert mlstm_w_in.shape[0] == 1 and gmlp_w_in.shape[0] == 1, "one mLSTM and one gMLP layer"
    h0 = x.reshape(batch * seq, d)
    h1, xn1 = _mlstm_layer(h0, mlstm_norm_g[0], mlstm_w_in[0], mlstm_conv_w[0], mlstm_conv_b[0],
                           mlstm_gate_b[0], mlstm_head_g[0], mlstm_w_out[0], gmlp_norm_g[0],
                           batch=batch, seq=seq)
    out = _gmlp_layer(h1, xn1, gmlp_w_in[0], gmlp_ln_g[0], gmlp_ln_b[0], gmlp_w_s[0],
                      gmlp_b_s[0], gmlp_w_out[0], final_norm_g)
    return out.reshape(batch, seq, d)
```

```python
import functools
import math

import jax
import jax.numpy as jnp
from jax import lax
from jax.experimental import pallas as pl
from jax.experimental.pallas import tpu as pltpu

EPS = 1e-6
F32 = jnp.float32
BF16 = jnp.bfloat16
LANES = 128
SUBLANES = 8
BF16_ROWS = 16
VMEM_LIMIT_BYTES = 56 * 1024 * 1024
MLSTM_CELL_CHUNK = 256
PROJ_TILE_M = 2048
PROJ_TILE_N = 1024
PROJ_SUB_ROWS = 256

_NT = (((1,), (1,)), ((), ()))


def _params(*sem):
    return pltpu.CompilerParams(dimension_semantics=sem, vmem_limit_bytes=VMEM_LIMIT_BYTES)


def _sigmoid(x):
    return 1.0 / (1.0 + jnp.exp2(x * (-math.log2(math.e))))


def _silu(x):
    return x * _sigmoid(x)


def _gelu_exact(x):
    return 0.5 * x * (1.0 + lax.erf(x * math.sqrt(0.5)))


def _log_sigmoid(x):
    return jnp.minimum(x, 0.0) - jnp.log1p(jnp.exp(-jnp.abs(x)))


_ACTIVATIONS = {"none": lambda x: x, "sigmoid": _sigmoid, "silu": _silu, "gelu": _gelu_exact}


def _pick(n, pref):
    tile = min(n, pref)
    while n % tile:
        tile //= 2
    return tile


def _norm_gate_kernel(x_ref, g_ref, wg_ref, gb_ref, xn_ref, rows_ref, *, heads):
    x = x_ref[...]
    xn = (x * lax.rsqrt(jnp.mean(x * x, axis=-1, keepdims=True) + EPS) * g_ref[...]).astype(BF16)
    xn_ref[...] = xn
    L = x.shape[0]
    gates = lax.dot_general(wg_ref[...].astype(BF16), xn, _NT,
                            preferred_element_type=F32) + gb_ref[...]
    ig = gates[0:heads]
    lf = _log_sigmoid(gates[heads:2 * heads])
    r = lax.broadcasted_iota(jnp.int32, (L, L), 0)
    c = lax.broadcasted_iota(jnp.int32, (L, L), 1)
    b = jnp.dot(lf, (r <= c).astype(F32), preferred_element_type=F32,
                precision=lax.Precision.HIGHEST)
    nc = ig - b
    lane = lax.broadcasted_iota(jnp.int32, nc.shape, 1)
    pm = nc
    shift = 1
    while shift < L:
        pm = jnp.maximum(pm, jnp.where(lane >= shift, pltpu.roll(pm, shift, axis=1), -jnp.inf))
        shift *= 2
    rows_ref[...] = jnp.concatenate([nc, b, pm], axis=0)


def _norm_gates(x, g, w_t, gate_b, *, chunk, gate_row0):
    t, d = x.shape
    heads = gate_b.shape[0] // 2
    assert heads % SUBLANES == 0 and gate_row0 % (2 * heads) == 0
    row = pl.BlockSpec((chunk, d), lambda i: (i, 0))
    return pl.pallas_call(
        functools.partial(_norm_gate_kernel, heads=heads),
        grid=(t // chunk,),
        in_specs=[row, pl.BlockSpec((1, d), lambda i: (0, 0)),
                  pl.BlockSpec((2 * heads, d), lambda i: (gate_row0 // (2 * heads), 0)),
                  pl.BlockSpec((2 * heads, 1), lambda i: (0, 0))],
        out_specs=[row, pl.BlockSpec((3 * heads, chunk), lambda i: (0, i))],
        out_shape=[jax.ShapeDtypeStruct((t, d), BF16),
                   jax.ShapeDtypeStruct((3 * heads, t), F32)],
        compiler_params=_params("arbitrary"),
        name="norm_gates",
    )(x, g.reshape(1, d), w_t, gate_b.reshape(2 * heads, 1))


def _weight_spec(k, tn, jb, w_is_nk):
    if w_is_nk:
        return pl.BlockSpec((tn, k), lambda j, i: (jb + j, 0)), (tn, k)
    return pl.BlockSpec((k, tn), lambda j, i: (0, jb + j)), (k, tn)


def _mm(x, w, w_is_nk):
    if w_is_nk:
        return lax.dot_general(x, w, _NT, preferred_element_type=F32)
    return jnp.dot(x, w, preferred_element_type=F32)


def _proj_kernel(x_ref, w_ref, *rest, act, w_is_nk, has_gain):
    gain_ref = rest[0] if has_gain else None
    o_ref, wbf_ref = rest[-2:]

    @pl.when(pl.program_id(1) == 0)
    def _():
        wbf_ref[...] = w_ref[...].astype(BF16)

    w = wbf_ref[...]
    for r0 in range(0, x_ref.shape[0], PROJ_SUB_ROWS):
        rows = slice(r0, r0 + PROJ_SUB_ROWS)
        out = _ACTIVATIONS[act](_mm(x_ref[rows, :], w, w_is_nk))
        if has_gain:
            out = out * gain_ref[...]
        o_ref[rows, :] = out.astype(o_ref.dtype)


def _proj(x, w, col0, n_out, *, act, w_is_nk, name, gain=None):
    t, k = x.shape
    tm, tn = _pick(t, PROJ_TILE_M), _pick(n_out, PROJ_TILE_N)
    assert col0 % tn == 0
    w_spec, w_block = _weight_spec(k, tn, col0 // tn, w_is_nk)
    in_specs = [pl.BlockSpec((tm, k), lambda j, i: (i, 0)), w_spec]
    operands = [x, w]
    if gain is not None:
        in_specs.append(pl.BlockSpec((1, tn), lambda j, i: (0, j)))
        operands.append(gain.reshape(1, n_out))
    return pl.pallas_call(
        functools.partial(_proj_kernel, act=act, w_is_nk=w_is_nk, has_gain=gain is not None),
        grid=(n_out // tn, t // tm),
        in_specs=in_specs,
        out_specs=pl.BlockSpec((tm, tn), lambda j, i: (i, j)),
        out_shape=jax.ShapeDtypeStruct((t, n_out), BF16),
        scratch_shapes=[pltpu.VMEM(w_block, BF16)],
        compiler_params=_params("arbitrary", "arbitrary"),
        name=name,
    )(*operands)


def _proj_conv_kernel(x_ref, xh_ref, w_ref, cw_ref, cb_ref, o_ref, wbf_ref,
                      *, seq, w_is_nk):
    i = pl.program_id(1)
    tm, tn = o_ref.shape
    conv_k = cw_ref.shape[0]
    groups = PROJ_SUB_ROWS // SUBLANES

    @pl.when(i == 0)
    def _():
        wbf_ref[...] = w_ref[...].astype(BF16)

    w = wbf_ref[...]
    prev = _mm(xh_ref[...], w, w_is_nk)[-SUBLANES:, :]
    prev = jnp.where((i * tm) % seq == 0, 0.0, prev)
    sub = lax.broadcasted_iota(jnp.int32, (groups, SUBLANES, tn), 1)
    for r0 in range(0, tm, PROJ_SUB_ROWS):
        rows = slice(r0, r0 + PROJ_SUB_ROWS)
        p = _mm(x_ref[rows, :], w, w_is_nk)
        acc = cb_ref[...] + cw_ref[conv_k - 1:conv_k, :] * p
        cur = p.reshape(groups, SUBLANES, tn)
        before = jnp.concatenate([prev[None], cur[:-1]], axis=0)
        for d in range(1, conv_k):
            mixed = jnp.where(sub >= SUBLANES - d, before, cur)
            shifted = pltpu.roll(mixed, d, axis=1).reshape(PROJ_SUB_ROWS, tn)
            acc = acc + cw_ref[conv_k - 1 - d:conv_k - d, :] * shifted
        o_ref[rows, :] = _silu(acc).astype(o_ref.dtype)
        prev = p[-SUBLANES:, :]


def _proj_conv(x, w, conv_w, conv_b, *, seq, w_is_nk, name):
    t, k = x.shape
    conv_k, n_out = conv_w.shape
    assert conv_k - 1 <= SUBLANES
    tm, tn = _pick(seq, PROJ_TILE_M), _pick(n_out // 2, PROJ_TILE_N)
    hb = tm // BF16_ROWS
    w_spec, w_block = _weight_spec(k, tn, 0, w_is_nk)
    kern = functools.partial(_proj_conv_kernel, seq=seq, w_is_nk=w_is_nk)
    return pl.pallas_call(
        kern,
        grid=(n_out // tn, t // tm),
        in_specs=[pl.BlockSpec((tm, k), lambda j, i: (i, 0)),
                  pl.BlockSpec((BF16_ROWS, k), lambda j, i: (jnp.maximum(i * hb - 1, 0), 0)),
                  w_spec,
                  pl.BlockSpec((conv_k, tn), lambda j, i: (0, j)),
                  pl.BlockSpec((1, tn), lambda j, i: (0, j))],
        out_specs=pl.BlockSpec((tm, tn), lambda j, i: (i, j)),
        out_shape=jax.ShapeDtypeStruct((t, n_out), BF16),
        scratch_shapes=[pltpu.VMEM(w_block, BF16)],
        compiler_params=_params("arbitrary", "arbitrary"),
        name=name,
    )(x, x, w, conv_w, conv_b.reshape(1, n_out))


def _mlstm_cell_kernel(qk_ref, v_ref, so_ref, szg_ref, rows_ref, y_ref, c_ref, n_ref, m_ref,
                       *, heads, dk, dv):
    L = qk_ref.shape[0]
    qk_w = heads * dk
    q_scale = dk ** -0.5

    @pl.when(pl.program_id(1) == 0)
    def _():
        c_ref[...] = jnp.zeros(c_ref.shape, F32)
        n_ref[...] = jnp.zeros(n_ref.shape, F32)
        m_ref[...] = jnp.zeros(m_ref.shape, F32)

    r = lax.broadcasted_iota(jnp.int32, (L, L), 0)
    c = lax.broadcasted_iota(jnp.int32, (L, L), 1)
    causal = r >= c
    ones_l = jnp.ones((SUBLANES, L), BF16)
    ones_v = jnp.ones((SUBLANES, dv), BF16)

    def as_rows(row, width):
        col = jnp.broadcast_to(row, (LANES, L)).T
        return jnp.concatenate([col] * (width // LANES), axis=1)

    def row_sum(ones, mat):
        return lax.dot_general(ones, mat, _NT, preferred_element_type=F32)[0:1, :]

    for h in range(heads):
        vcols = slice(h * dv, (h + 1) * dv)
        q = qk_ref[:, h * dk:(h + 1) * dk]
        k = qk_ref[:, qk_w + h * dk:qk_w + (h + 1) * dk]
        v = v_ref[:, vcols]
        nc = rows_ref[h:h + 1, :]
        b = rows_ref[heads + h:heads + h + 1, :]
        pm = rows_ref[2 * heads + h:2 * heads + h + 1, :]
        m = m_ref[h:h + 1, 0:1]

        u = jnp.maximum(m, pm)
        a = jnp.exp(m - u)
        dexp = jnp.exp(jnp.where(causal, nc - as_rows(u, L), -jnp.inf))
        s = (lax.dot_general(q, k, _NT, preferred_element_type=F32) * dexp).astype(BF16)
        num = (as_rows(a, dv) * jnp.dot(q, c_ref[h].astype(BF16), preferred_element_type=F32)
               + jnp.dot(s, v, preferred_element_type=F32))
        n_rows = jnp.broadcast_to(n_ref[h:h + 1, :], (SUBLANES, dk)).astype(BF16)
        den = (a * row_sum(n_rows, q) + row_sum(ones_l, s)) * q_scale
        rec = q_scale / jnp.maximum(jnp.abs(den), jnp.exp(-(b + u)))

        t = num * so_ref[:, vcols].astype(F32)
        rms = rec * jnp.sqrt(row_sum(ones_v, (t * t).astype(BF16)) * (1.0 / dv))
        scale = rec * lax.rsqrt(rms * rms + EPS)
        y_ref[:, vcols] = (t * as_rows(scale, dv)
                           * szg_ref[:, vcols].astype(F32)).astype(y_ref.dtype)

        g = b[:, L - 1:L]
        m_new = g + jnp.maximum(m, pm[:, L - 1:L])
        kw_t = (k.T.astype(F32) * jnp.exp((g + nc) - m_new)).astype(BF16)
        decay = jnp.exp(g + m - m_new)
        c_ref[h] = decay * c_ref[h] + jnp.dot(kw_t, v, preferred_element_type=F32)
        n_ref[h:h + 1, :] = decay * n_ref[h:h + 1, :] + row_sum(ones_l, kw_t)
        m_ref[h:h + 1, :] = jnp.broadcast_to(m_new, (1, m_ref.shape[1]))


def _mlstm_cell(qk, v, so, szg, rows, *, batch, seq, heads, chunk):
    t, inner = v.shape
    qk_w = qk.shape[1] // 2
    dk, dv = qk_w // heads, inner // heads
    nc = seq // chunk
    row = lambda b, c: b * nc + c
    wide = lambda width: pl.BlockSpec((chunk, width), lambda b, c: (row(b, c), 0))
    kern = functools.partial(_mlstm_cell_kernel, heads=heads, dk=dk, dv=dv)
    return pl.pallas_call(
        kern,
        grid=(batch, nc),
        in_specs=[wide(2 * qk_w), wide(inner), wide(inner), wide(inner),
                  pl.BlockSpec((3 * heads, chunk), lambda b, c: (0, row(b, c)))],
        out_specs=wide(inner),
        out_shape=jax.ShapeDtypeStruct((t, inner), BF16),
        scratch_shapes=[pltpu.VMEM((heads, dk, dv), F32),
                        pltpu.VMEM((heads, dk), F32),
                        pltpu.VMEM((heads, LANES), F32)],
        compiler_params=_params("arbitrary", "arbitrary"),
        name="mlstm_cell",
    )(qk, v, so, szg, rows)


def _outproj_kernel(y_ref, w_ref, r_ref, g_ref, *out_refs, emit_residual):
    h = r_ref[...] + jnp.dot(y_ref[...], w_ref[...], preferred_element_type=F32)
    xn = h * lax.rsqrt(jnp.mean(h * h, axis=-1, keepdims=True) + EPS) * g_ref[...]
    if emit_residual:
        h_ref, xn_ref = out_refs
        h_ref[...] = h
    else:
        (xn_ref,) = out_refs
    xn_ref[...] = xn.astype(xn_ref.dtype)


def _outproj(y, w, resid, g, *, tm, emit_residual, norm_dtype, name):
    t, k = y.shape
    d = w.shape[1]
    row = lambda width: pl.BlockSpec((tm, width), lambda i: (i, 0))
    out_specs = [row(d)]
    out_shape = [jax.ShapeDtypeStruct((t, d), norm_dtype)]
    if emit_residual:
        out_specs = [row(d)] + out_specs
        out_shape = [jax.ShapeDtypeStruct((t, d), F32)] + out_shape
    return pl.pallas_call(
        functools.partial(_outproj_kernel, emit_residual=emit_residual),
        grid=(t // tm,),
        in_specs=[row(k), pl.BlockSpec((k, d), lambda i: (0, 0)), row(d),
                  pl.BlockSpec((1, d), lambda i: (0, 0))],
        out_specs=out_specs,
        out_shape=out_shape,
        compiler_params=_params("arbitrary"),
        name=name,
    )(y, w, resid, g.reshape(1, d))


def _gmlp_mix_kernel(u_ref, v_ref, sz_ref, lg_ref, lb_ref, ws_ref, bs_ref, y_ref, vn_ref,
                     *, groups):
    tm, inner = v_ref.shape
    ch = ws_ref.shape[1]
    gd = inner // groups
    v = v_ref[...].astype(F32)
    mu = jnp.mean(v, axis=-1, keepdims=True)
    d = v - mu
    var = jnp.mean(d * d, axis=-1, keepdims=True)
    vn_ref[...] = (d * lax.rsqrt(var + EPS) * lg_ref[...] + lb_ref[...]).astype(vn_ref.dtype)

    r = lax.broadcasted_iota(jnp.int32, (ch, ch), 0)
    c = lax.broadcasted_iota(jnp.int32, (ch, ch), 1)
    mask = (r >= c).astype(F32)
    for g in range(groups):
        ws = (ws_ref[g] * mask).astype(BF16)
        cols = slice(g * gd, (g + 1) * gd)
        for k in range(tm // ch):
            rows = slice(k * ch, (k + 1) * ch)
            sv = jnp.dot(ws, vn_ref[rows, cols], preferred_element_type=F32) + bs_ref[:, g:g + 1]
            u = u_ref[rows, cols].astype(F32)
            y_ref[rows, cols] = (u * sv * sz_ref[rows, cols].astype(F32)).astype(y_ref.dtype)


def _gmlp_mix(uv, sz, ln_g, ln_b, w_s, b_s_t, *, tm):
    t, inner = sz.shape
    groups, ch, _ = w_s.shape
    wide = lambda j: pl.BlockSpec((tm, inner), lambda i: (i, j))
    full = lambda shape: pl.BlockSpec(shape, lambda i: (0,) * len(shape))
    return pl.pallas_call(
        functools.partial(_gmlp_mix_kernel, groups=groups),
        grid=(t // tm,),
        in_specs=[wide(0), wide(1), wide(0), full((1, inner)), full((1, inner)),
                  full((groups, ch, ch)), full((ch, groups))],
        out_specs=pl.BlockSpec((tm, inner), lambda i: (i, 0)),
        out_shape=jax.ShapeDtypeStruct((t, inner), BF16),
        scratch_shapes=[pltpu.VMEM((tm, inner), BF16)],
        compiler_params=_params("arbitrary"),
        name="gmlp_mix",
    )(uv, uv, sz, ln_g.reshape(1, inner), ln_b.reshape(1, inner), w_s, b_s_t)


def _mlstm_layer(h2d, norm_g, w_in, conv_w, conv_b, gate_b, head_g, w_out, next_g,
                 *, batch, seq):
    t, d = h2d.shape
    heads = gate_b.shape[0] // 2
    qk_w = conv_w.shape[1] // 2
    inner = head_g.shape[0]
    n_main = 2 * qk_w + 3 * inner
    chunk = _pick(seq, MLSTM_CELL_CHUNK)

    w_t = jnp.transpose(w_in)
    xn, rows = _norm_gates(h2d, norm_g, w_t, gate_b, chunk=chunk, gate_row0=n_main)

    proj = functools.partial(_proj, xn, w_t, w_is_nk=True)
    qk = _proj_conv(xn, w_t, conv_w, conv_b, seq=seq, w_is_nk=True, name="mlstm_proj_qk")
    v = proj(2 * qk_w, inner, act="none", name="mlstm_proj_v")
    so = proj(2 * qk_w + inner, inner, act="sigmoid", name="mlstm_proj_o")
    szg = proj(2 * qk_w + 2 * inner, inner, act="silu", gain=head_g, name="mlstm_proj_z")

    y = _mlstm_cell(qk, v, so, szg, rows, batch=batch, seq=seq, heads=heads, chunk=chunk)
    return _outproj(y, w_out.astype(BF16), h2d, next_g, tm=_pick(t, 256),
                    emit_residual=True, norm_dtype=BF16, name="mlstm_outproj")


def _gmlp_layer(h2d, xn, w_in, ln_g, ln_b, w_s, b_s, w_out, final_g):
    t, d = h2d.shape
    inner = ln_g.shape[0]
    ch = w_s.shape[1]
    proj = functools.partial(_proj, xn, w_in, w_is_nk=False)
    uv = proj(0, 2 * inner, act="gelu", name="gmlp_proj_uv")
    sz = proj(2 * inner, inner, act="silu", name="gmlp_proj_z")
    y = _gmlp_mix(uv, sz, ln_g, ln_b, w_s, jnp.transpose(b_s), tm=max(ch, _pick(t, 256)))
    (out,) = _outproj(y, w_out.astype(BF16), h2d, final_g, tm=_pick(t, 256),
                      emit_residual=False, norm_dtype=F32, name="gmlp_outproj")
    return out


def kernel(x, mlstm_norm_g, mlstm_w_in, mlstm_conv_w, mlstm_conv_b, mlstm_gate_b, mlstm_head_g,
           mlstm_w_out, gmlp_norm_g, gmlp_w_in, gmlp_ln_g, gmlp_ln_b, gmlp_w_s, gmlp_b_s,
           gmlp_w_out, final_norm_g):
    batch, seq, d = x.shape
    assert mlstm_w_in.shape[0] == 1 and gmlp_w_in.shape[0] == 1, "one mLSTM and one gMLP layer"
    h0 = x.reshape(batch * seq, d)
    h1, xn1 = _mlstm_layer(h0, mlstm_norm_g[0], mlstm_w_in[0], mlstm_conv_w[0], mlstm_conv_b[0],
                           mlstm_gate_b[0], mlstm_head_g[0], mlstm_w_out[0], gmlp_norm_g[0],
                           batch=batch, seq=seq)
    out = _gmlp_layer(h1, xn1, gmlp_w_in[0], gmlp_ln_g[0], gmlp_ln_b[0], gmlp_w_s[0],
                      gmlp_b_s[0], gmlp_w_out[0], final_norm_g)
    return out.reshape(batch, seq, d)
```

```python
import functools
import math

import jax
import jax.numpy as jnp
from jax import lax
from jax.experimental import pallas as pl
from jax.experimental.pallas import tpu as pltpu

EPS = 1e-6
F32 = jnp.float32
BF16 = jnp.bfloat16
LANES = 128
SUBLANES = 8
BF16_ROWS = 16
VMEM_LIMIT_BYTES = 56 * 1024 * 1024
MLSTM_CELL_CHUNK = 256
GATE_TILE_M = 1024
PROJ_TILE_M = 2048
PROJ_TILE_N = 1024
PROJ_SUB_ROWS = 256

_NT = (((1,), (1,)), ((), ()))


def _params(*sem):
    return pltpu.CompilerParams(dimension_semantics=sem, vmem_limit_bytes=VMEM_LIMIT_BYTES)


def _sigmoid(x):
    return 1.0 / (1.0 + jnp.exp2(x * (-math.log2(math.e))))


def _silu(x):
    return x * _sigmoid(x)


def _gelu_exact(x):
    return 0.5 * x * (1.0 + lax.erf(x * math.sqrt(0.5)))


def _log_sigmoid(x):
    return jnp.minimum(x, 0.0) - jnp.log1p(jnp.exp(-jnp.abs(x)))


_ACTIVATIONS = {"none": lambda x: x, "sigmoid": _sigmoid, "silu": _silu, "gelu": _gelu_exact}


def _pick(n, pref):
    tile = min(n, pref)
    while n % tile:
        tile //= 2
    return tile


def _norm_gate_kernel(x_ref, g_ref, wg_ref, gb_ref, xn_ref, rows_ref, *, heads, chunk):
    x = x_ref[...]
    xn = (x * lax.rsqrt(jnp.mean(x * x, axis=-1, keepdims=True) + EPS) * g_ref[...]).astype(BF16)
    xn_ref[...] = xn
    L = chunk
    n_chunks = x.shape[0] // L
    gates = lax.dot_general(wg_ref[...].astype(BF16), xn, _NT,
                            preferred_element_type=F32) + gb_ref[...]

    def stack(g):
        return jnp.concatenate([g[:, k * L:(k + 1) * L] for k in range(n_chunks)], axis=0)

    ig = stack(gates[0:heads])
    lf = _log_sigmoid(stack(gates[heads:2 * heads]))
    r = lax.broadcasted_iota(jnp.int32, (L, L), 0)
    c = lax.broadcasted_iota(jnp.int32, (L, L), 1)
    b = jnp.dot(lf, (r <= c).astype(F32), preferred_element_type=F32,
                precision=lax.Precision.HIGHEST)
    nc = ig - b
    lane = lax.broadcasted_iota(jnp.int32, nc.shape, 1)
    pm = nc
    shift = 1
    while shift < L:
        pm = jnp.maximum(pm, jnp.where(lane >= shift, pltpu.roll(pm, shift, axis=1), -jnp.inf))
        shift *= 2
    for k in range(n_chunks):
        rs = slice(k * heads, (k + 1) * heads)
        rows_ref[:, k * L:(k + 1) * L] = jnp.concatenate([nc[rs], b[rs], pm[rs]], axis=0)


def _norm_gates(x, g, w_t, gate_b, *, chunk, gate_row0):
    t, d = x.shape
    heads = gate_b.shape[0] // 2
    assert heads % SUBLANES == 0 and gate_row0 % (2 * heads) == 0
    tm = max(chunk, _pick(t, GATE_TILE_M))
    row = pl.BlockSpec((tm, d), lambda i: (i, 0))
    return pl.pallas_call(
        functools.partial(_norm_gate_kernel, heads=heads, chunk=chunk),
        grid=(t // tm,),
        in_specs=[row, pl.BlockSpec((1, d), lambda i: (0, 0)),
                  pl.BlockSpec((2 * heads, d), lambda i: (gate_row0 // (2 * heads), 0)),
                  pl.BlockSpec((2 * heads, 1), lambda i: (0, 0))],
        out_specs=[row, pl.BlockSpec((3 * heads, tm), lambda i: (0, i))],
        out_shape=[jax.ShapeDtypeStruct((t, d), BF16),
                   jax.ShapeDtypeStruct((3 * heads, t), F32)],
        compiler_params=_params("arbitrary"),
        name="norm_gates",
    )(x, g.reshape(1, d), w_t, gate_b.reshape(2 * heads, 1))


def _weight_spec(k, tn, jb, w_is_nk):
    if w_is_nk:
        return pl.BlockSpec((tn, k), lambda j, i: (jb + j, 0)), (tn, k)
    return pl.BlockSpec((k, tn), lambda j, i: (0, jb + j)), (k, tn)


def _mm(x, w, w_is_nk):
    if w_is_nk:
        return lax.dot_general(x, w, _NT, preferred_element_type=F32)
    return jnp.dot(x, w, preferred_element_type=F32)


def _proj_kernel(x_ref, w_ref, *rest, act, w_is_nk, has_gain):
    gain_ref = rest[0] if has_gain else None
    o_ref, wbf_ref = rest[-2:]

    @pl.when(pl.program_id(1) == 0)
    def _():
        wbf_ref[...] = w_ref[...].astype(BF16)

    w = wbf_ref[...]
    for r0 in range(0, x_ref.shape[0], PROJ_SUB_ROWS):
        rows = slice(r0, r0 + PROJ_SUB_ROWS)
        out = _ACTIVATIONS[act](_mm(x_ref[rows, :], w, w_is_nk))
        if has_gain:
            out = out * gain_ref[...]
        o_ref[rows, :] = out.astype(o_ref.dtype)


def _proj(x, w, col0, n_out, *, act, w_is_nk, name, gain=None):
    t, k = x.shape
    tm, tn = _pick(t, PROJ_TILE_M), _pick(n_out, PROJ_TILE_N)
    assert col0 % tn == 0
    w_spec, w_block = _weight_spec(k, tn, col0 // tn, w_is_nk)
    in_specs = [pl.BlockSpec((tm, k), lambda j, i: (i, 0)), w_spec]
    operands = [x, w]
    if gain is not None:
        in_specs.append(pl.BlockSpec((1, tn), lambda j, i: (0, j)))
        operands.append(gain.reshape(1, n_out))
    return pl.pallas_call(
        functools.partial(_proj_kernel, act=act, w_is_nk=w_is_nk, has_gain=gain is not None),
        grid=(n_out // tn, t // tm),
        in_specs=in_specs,
        out_specs=pl.BlockSpec((tm, tn), lambda j, i: (i, j)),
        out_shape=jax.ShapeDtypeStruct((t, n_out), BF16),
        scratch_shapes=[pltpu.VMEM(w_block, BF16)],
        compiler_params=_params("arbitrary", "arbitrary"),
        name=name,
    )(*operands)


def _proj_conv_kernel(x_ref, xh_ref, w_ref, cw_ref, cb_ref, o_ref, wbf_ref,
                      *, seq, w_is_nk):
    i = pl.program_id(1)
    tm, tn = o_ref.shape
    conv_k = cw_ref.shape[0]
    groups = PROJ_SUB_ROWS // SUBLANES

    @pl.when(i == 0)
    def _():
        wbf_ref[...] = w_ref[...].astype(BF16)

    w = wbf_ref[...]
    prev = _mm(xh_ref[...], w, w_is_nk)[-SUBLANES:, :]
    prev = jnp.where((i * tm) % seq == 0, 0.0, prev)
    sub = lax.broadcasted_iota(jnp.int32, (groups, SUBLANES, tn), 1)
    for r0 in range(0, tm, PROJ_SUB_ROWS):
        rows = slice(r0, r0 + PROJ_SUB_ROWS)
        p = _mm(x_ref[rows, :], w, w_is_nk)
        acc = cb_ref[...] + cw_ref[conv_k - 1:conv_k, :] * p
        cur = p.reshape(groups, SUBLANES, tn)
        before = jnp.concatenate([prev[None], cur[:-1]], axis=0)
        for d in range(1, conv_k):
            mixed = jnp.where(sub >= SUBLANES - d, before, cur)
            shifted = pltpu.roll(mixed, d, axis=1).reshape(PROJ_SUB_ROWS, tn)
            acc = acc + cw_ref[conv_k - 1 - d:conv_k - d, :] * shifted
        o_ref[rows, :] = _silu(acc).astype(o_ref.dtype)
        prev = p[-SUBLANES:, :]


def _proj_conv(x, w, conv_w, conv_b, *, seq, w_is_nk, name):
    t, k = x.shape
    conv_k, n_out = conv_w.shape
    assert conv_k - 1 <= SUBLANES
    tm, tn = _pick(seq, PROJ_TILE_M), _pick(n_out // 2, PROJ_TILE_N)
    hb = tm // BF16_ROWS
    w_spec, w_block = _weight_spec(k, tn, 0, w_is_nk)
    kern = functools.partial(_proj_conv_kernel, seq=seq, w_is_nk=w_is_nk)
    return pl.pallas_call(
        kern,
        grid=(n_out // tn, t // tm),
        in_specs=[pl.BlockSpec((tm, k), lambda j, i: (i, 0)),
                  pl.BlockSpec((BF16_ROWS, k), lambda j, i: (jnp.maximum(i * hb - 1, 0), 0)),
                  w_spec,
                  pl.BlockSpec((conv_k, tn), lambda j, i: (0, j)),
                  pl.BlockSpec((1, tn), lambda j, i: (0, j))],
        out_specs=pl.BlockSpec((tm, tn), lambda j, i: (i, j)),
        out_shape=jax.ShapeDtypeStruct((t, n_out), BF16),
        scratch_shapes=[pltpu.VMEM(w_block, BF16)],
        compiler_params=_params("arbitrary", "arbitrary"),
        name=name,
    )(x, x, w, conv_w, conv_b.reshape(1, n_out))


def _mlstm_cell_kernel(qk_ref, v_ref, so_ref, szg_ref, rows_ref, y_ref, c_ref, n_ref, m_ref,
                       *, heads, dk, dv):
    L = qk_ref.shape[0]
    qk_w = heads * dk
    q_scale = dk ** -0.5

    @pl.when(pl.program_id(1) == 0)
    def _():
        c_ref[...] = jnp.zeros(c_ref.shape, F32)
        n_ref[...] = jnp.zeros(n_ref.shape, F32)
        m_ref[...] = jnp.zeros(m_ref.shape, F32)

    r = lax.broadcasted_iota(jnp.int32, (L, L), 0)
    c = lax.broadcasted_iota(jnp.int32, (L, L), 1)
    causal = r >= c
    ones_l = jnp.ones((SUBLANES, L), BF16)
    ones_v = jnp.ones((SUBLANES, dv), BF16)

    def as_rows(row, width):
        col = jnp.broadcast_to(row, (LANES, L)).T
        return jnp.concatenate([col] * (width // LANES), axis=1)

    def row_sum(ones, mat):
        return lax.dot_general(ones, mat, _NT, preferred_element_type=F32)[0:1, :]

    for h in range(heads):
        vcols = slice(h * dv, (h + 1) * dv)
        q = qk_ref[:, h * dk:(h + 1) * dk]
        k = qk_ref[:, qk_w + h * dk:qk_w + (h + 1) * dk]
        v = v_ref[:, vcols]
        nc = rows_ref[h:h + 1, :]
        b = rows_ref[heads + h:heads + h + 1, :]
        pm = rows_ref[2 * heads + h:2 * heads + h + 1, :]
        m = m_ref[h:h + 1, 0:1]

        u = jnp.maximum(m, pm)
        a = jnp.exp(m - u)
        dexp = jnp.exp(jnp.where(causal, nc - as_rows(u, L), -jnp.inf))
        s = (lax.dot_general(q, k, _NT, preferred_element_type=F32) * dexp).astype(BF16)
        num = (as_rows(a, dv) * jnp.dot(q, c_ref[h].astype(BF16), preferred_element_type=F32)
               + jnp.dot(s, v, preferred_element_type=F32))
        n_rows = jnp.broadcast_to(n_ref[h:h + 1, :], (SUBLANES, dk)).astype(BF16)
        den = (a * row_sum(n_rows, q) + row_sum(ones_l, s)) * q_scale
        rec = q_scale / jnp.maximum(jnp.abs(den), jnp.exp(-(b + u)))

        t = num * so_ref[:, vcols].astype(F32)
        rms = rec * jnp.sqrt(row_sum(ones_v, (t * t).astype(BF16)) * (1.0 / dv))
        scale = rec * lax.rsqrt(rms * rms + EPS)
        y_ref[:, vcols] = (t * as_rows(scale, dv)
                           * szg_ref[:, vcols].astype(F32)).astype(y_ref.dtype)

        g = b[:, L - 1:L]
        m_new = g + jnp.maximum(m, pm[:, L - 1:L])
        kw_t = (k.T.astype(F32) * jnp.exp((g + nc) - m_new)).astype(BF16)
        decay = jnp.exp(g + m - m_new)
        c_ref[h] = decay * c_ref[h] + jnp.dot(kw_t, v, preferred_element_type=F32)
        n_ref[h:h + 1, :] = decay * n_ref[h:h + 1, :] + row_sum(ones_l, kw_t)
        m_ref[h:h + 1, :] = jnp.broadcast_to(m_new, (1, m_ref.shape[1]))


def _mlstm_cell(qk, v, so, szg, rows, *, batch, seq, heads, chunk):
    t, inner = v.shape
    qk_w = qk.shape[1] // 2
    dk, dv = qk_w // heads, inner // heads
    nc = seq // chunk
    row = lambda b, c: b * nc + c
    wide = lambda width: pl.BlockSpec((chunk, width), lambda b, c: (row(b, c), 0))
    kern = functools.partial(_mlstm_cell_kernel, heads=heads, dk=dk, dv=dv)
    return pl.pallas_call(
        kern,
        grid=(batch, nc),
        in_specs=[wide(2 * qk_w), wide(inner), wide(inner), wide(inner),
                  pl.BlockSpec((3 * heads, chunk), lambda b, c: (0, row(b, c)))],
        out_specs=wide(inner),
        out_shape=jax.ShapeDtypeStruct((t, inner), BF16),
        scratch_shapes=[pltpu.VMEM((heads, dk, dv), F32),
                        pltpu.VMEM((heads, dk), F32),
                        pltpu.VMEM((heads, LANES), F32)],
        compiler_params=_params("arbitrary", "arbitrary"),
        name="mlstm_cell",
    )(qk, v, so, szg, rows)


def _outproj_kernel(y_ref, w_ref, r_ref, g_ref, *out_refs, emit_residual):
    h = r_ref[...] + jnp.dot(y_ref[...], w_ref[...], preferred_element_type=F32)
    xn = h * lax.rsqrt(jnp.mean(h * h, axis=-1, keepdims=True) + EPS) * g_ref[...]
    if emit_residual:
        h_ref, xn_ref = out_refs
        h_ref[...] = h
    else:
        (xn_ref,) = out_refs
    xn_ref[...] = xn.astype(xn_ref.dtype)


def _outproj(y, w, resid, g, *, tm, emit_residual, norm_dtype, name):
    t, k = y.shape
    d = w.shape[1]
    row = lambda width: pl.BlockSpec((tm, width), lambda i: (i, 0))
    out_specs = [row(d)]
    out_shape = [jax.ShapeDtypeStruct((t, d), norm_dtype)]
    if emit_residual:
        out_specs = [row(d)] + out_specs
        out_shape = [jax.ShapeDtypeStruct((t, d), F32)] + out_shape
    return pl.pallas_call(
        functools.partial(_outproj_kernel, emit_residual=emit_residual),
        grid=(t // tm,),
        in_specs=[row(k), pl.BlockSpec((k, d), lambda i: (0, 0)), row(d),
                  pl.BlockSpec((1, d), lambda i: (0, 0))],
        out_specs=out_specs,
        out_shape=out_shape,
        compiler_params=_params("arbitrary"),
        name=name,
    )(y, w, resid, g.reshape(1, d))


def _gmlp_mix_kernel(u_ref, v_ref, sz_ref, lg_ref, lb_ref, ws_ref, bs_ref, y_ref, vn_ref,
                     *, groups):
    tm, inner = v_ref.shape
    ch = ws_ref.shape[1]
    gd = inner // groups
    v = v_ref[...].astype(F32)
    mu = jnp.mean(v, axis=-1, keepdims=True)
    d = v - mu
    var = jnp.mean(d * d, axis=-1, keepdims=True)
    vn_ref[...] = (d * lax.rsqrt(var + EPS) * lg_ref[...] + lb_ref[...]).astype(vn_ref.dtype)

    r = lax.broadcasted_iota(jnp.int32, (ch, ch), 0)
    c = lax.broadcasted_iota(jnp.int32, (ch, ch), 1)
    mask = (r >= c).astype(F32)
    for g in range(groups):
        ws = (ws_ref[g] * mask).astype(BF16)
        cols = slice(g * gd, (g + 1) * gd)
        for k in range(tm // ch):
            rows = slice(k * ch, (k + 1) * ch)
            sv = jnp.dot(ws, vn_ref[rows, cols], preferred_element_type=F32) + bs_ref[:, g:g + 1]
            u = u_ref[rows, cols].astype(F32)
            y_ref[rows, cols] = (u * sv * sz_ref[rows, cols].astype(F32)).astype(y_ref.dtype)


def _gmlp_mix(uv, sz, ln_g, ln_b, w_s, b_s_t, *, tm):
    t, inner = sz.shape
    groups, ch, _ = w_s.shape
    wide = lambda j: pl.BlockSpec((tm, inner), lambda i: (i, j))
    full = lambda shape: pl.BlockSpec(shape, lambda i: (0,) * len(shape))
    return pl.pallas_call(
        functools.partial(_gmlp_mix_kernel, groups=groups),
        grid=(t // tm,),
        in_specs=[wide(0), wide(1), wide(0), full((1, inner)), full((1, inner)),
                  full((groups, ch, ch)), full((ch, groups))],
        out_specs=pl.BlockSpec((tm, inner), lambda i: (i, 0)),
        out_shape=jax.ShapeDtypeStruct((t, inner), BF16),
        scratch_shapes=[pltpu.VMEM((tm, inner), BF16)],
        compiler_params=_params("arbitrary"),
        name="gmlp_mix",
    )(uv, uv, sz, ln_g.reshape(1, inner), ln_b.reshape(1, inner), w_s, b_s_t)


def _mlstm_layer(h2d, norm_g, w_in, conv_w, conv_b, gate_b, head_g, w_out, next_g,
                 *, batch, seq):
    t, d = h2d.shape
    heads = gate_b.shape[0] // 2
    qk_w = conv_w.shape[1] // 2
    inner = head_g.shape[0]
    n_main = 2 * qk_w + 3 * inner
    chunk = _pick(seq, MLSTM_CELL_CHUNK)

    w_t = jnp.transpose(w_in)
    xn, rows = _norm_gates(h2d, norm_g, w_t, gate_b, chunk=chunk, gate_row0=n_main)

    proj = functools.partial(_proj, xn, w_t, w_is_nk=True)
    qk = _proj_conv(xn, w_t, conv_w, conv_b, seq=seq, w_is_nk=True, name="mlstm_proj_qk")
    v = proj(2 * qk_w, inner, act="none", name="mlstm_proj_v")
    so = proj(2 * qk_w + inner, inner, act="sigmoid", name="mlstm_proj_o")
    szg = proj(2 * qk_w + 2 * inner, inner, act="silu", gain=head_g, name="mlstm_proj_z")

    y = _mlstm_cell(qk, v, so, szg, rows, batch=batch, seq=seq, heads=heads, chunk=chunk)
    return _outproj(y, w_out.astype(BF16), h2d, next_g, tm=_pick(t, 256),
                    emit_residual=True, norm_dtype=BF16, name="mlstm_outproj")


def _gmlp_layer(h2d, xn, w_in, ln_g, ln_b, w_s, b_s, w_out, final_g):
    t, d = h2d.shape
    inner = ln_g.shape[0]
    ch = w_s.shape[1]
    proj = functools.partial(_proj, xn, w_in, w_is_nk=False)
    uv = proj(0, 2 * inner, act="gelu", name="gmlp_proj_uv")
    sz = proj(2 * inner, inner, act="silu", name="gmlp_proj_z")
    y = _gmlp_mix(uv, sz, ln_g, ln_b, w_s, jnp.transpose(b_s), tm=max(ch, _pick(t, 256)))
    (out,) = _outproj(y, w_out.astype(BF16), h2d, final_g, tm=_pick(t, 256),
                      emit_residual=False, norm_dtype=F32, name="gmlp_outproj")
    return out


def kernel(x, mlstm_norm_g, mlstm_w_in, mlstm_conv_w, mlstm_conv_b, mlstm_gate_b, mlstm_head_g,
           mlstm_w_out, gmlp_norm_g, gmlp_w_in, gmlp_ln_g, gmlp_ln_b, gmlp_w_s, gmlp_b_s,
           gmlp_w_out, final_norm_g):
    batch, seq, d = x.shape
    assert mlstm_w_in.shape[0] == 1 and gmlp_w_in.shape[0] == 1, "one mLSTM and one gMLP layer"
    h0 = x.reshape(batch * seq, d)
    h1, xn1 = _mlstm_layer(h0, mlstm_norm_g[0], mlstm_w_in[0], mlstm_conv_w[0], mlstm_conv_b[0],
                           mlstm_gate_b[0], mlstm_head_g[0], mlstm_w_out[0], gmlp_norm_g[0],
                           batch=batch, seq=seq)
    out = _gmlp_layer(h1, xn1, gmlp_w_in[0], gmlp_ln_g[0], gmlp_ln_b[0], gmlp_w_s[0],
                      gmlp_b_s[0], gmlp_w_out[0], final_norm_g)
    return out.reshape(batch, seq, d)
```
